```python
import jax, jax.numpy as jnp
from jax import lax
import numpy as np

D_MODEL = 1024
BATCH = 8
SEQ = 4096
DEPTH = 1
DEC_BATCH = 16
DEC_SEQ = 16
PAST_LEN = 2048

CHUNK = 64
N_META = 16
M_HEADS = 4
D_M = D_MODEL
M_HEAD_DIM = D_M // M_HEADS
D_C = D_MODEL
CONV_W = 31
D_FF = 4 * D_MODEL
EPS = 1e-6

OFF_Q = 0
OFF_K = OFF_Q + D_M
OFF_V = OFF_K + D_M
OFF_O = OFF_V + D_M
OFF_I = OFF_O + D_M
OFF_F = OFF_I + M_HEADS
OFF_CA = OFF_F + M_HEADS
OFF_CB = OFF_CA + D_C
OFF_GM = OFF_CB + D_C
OFF_GC = OFF_GM + D_MODEL
N_IN = OFF_GC + D_MODEL

kernel_name = "mlstm_conformer_gated_stream_step"


def rmsnorm(x, g):
    xf = x.astype(jnp.float32)
    y = xf * lax.rsqrt(jnp.mean(xf * xf, axis=-1, keepdims=True) + EPS)
    return (y * g.astype(jnp.float32)).astype(x.dtype)


def layernorm(x, g, b):
    xf = x.astype(jnp.float32)
    mu = jnp.mean(xf, axis=-1, keepdims=True)
    xc = xf - mu
    y = xc * lax.rsqrt(jnp.mean(xc * xc, axis=-1, keepdims=True) + EPS)
    return (y * g.astype(jnp.float32) + b.astype(jnp.float32)).astype(x.dtype)


def mlstm_chunk(carry, blk):
    C0, n0, m0 = carry
    q, k, v, it, lf = blk
    L = q.shape[2]
    b = jnp.cumsum(lf, axis=-1)
    a = b + m0[..., None]
    causal = jnp.tril(jnp.ones((L, L), dtype=bool))
    dmat = jnp.where(causal, b[..., :, None] - b[..., None, :] + it[..., None, :], -jnp.inf)
    m = jnp.maximum(a, jnp.max(dmat, axis=-1))
    w_state = jnp.exp(a - m)
    w_intra = jnp.exp(dmat - m[..., None])
    p = w_intra * jnp.einsum('bhtd,bhsd->bhts', q, k)
    num = jnp.einsum('bhts,bhse->bhte', p, v) + w_state[..., None] * jnp.einsum('bhed,bhtd->bhte', C0, q)
    den = jnp.sum(p, axis=-1) + w_state * jnp.einsum('bhd,bhtd->bht', n0, q)
    h = num / jnp.maximum(jnp.abs(den), jnp.exp(-m))[..., None]
    m_last = m[..., -1]
    w_last = w_state[..., -1]
    ws = w_intra[..., -1, :]
    C1 = w_last[..., None, None] * C0 + jnp.einsum('bhs,bhse,bhsd->bhed', ws, v, k)
    n1 = w_last[..., None] * n0 + jnp.einsum('bhs,bhsd->bhd', ws, k)
    return (C1, n1, m_last), h


def mlstm_sequence(carry, blk, n_lead):
    if n_lead == 0:
        return mlstm_chunk(carry, blk)
    lead = tuple(t[:, :, :n_lead] for t in blk)
    carry, h_lead = mlstm_chunk(carry, lead)
    rest = tuple(t[:, :, n_lead:] for t in blk)
    n_chunks = rest[0].shape[2] // CHUNK

    def to_chunks(t):
        t = t.reshape(t.shape[:2] + (n_chunks, CHUNK) + t.shape[3:])
        return jnp.moveaxis(t, 2, 0)

    carry, h_rest = lax.scan(mlstm_chunk, carry, tuple(to_chunks(t) for t in rest))
    h_rest = jnp.moveaxis(h_rest, 0, 2)
    h_rest = h_rest.reshape(h_rest.shape[:2] + (n_chunks * CHUNK, M_HEAD_DIM))
    return carry, jnp.concatenate([h_lead, h_rest], axis=2)


def depthwise_conv_valid(padded, w, bias):
    y = lax.conv_general_dilated(padded, w[:, None, :].astype(padded.dtype), window_strides=(1,),
                                 padding='VALID', dimension_numbers=('NWC', 'WIO', 'NWC'),
                                 feature_group_count=padded.shape[-1])
    return y + bias.astype(padded.dtype)


def layer(x, C0, n0, m0, conv_buf, n_lead, g_pre_mix, w_in, b_if, g_headnorm, w_branch_m,
          conv_w, conv_b, ln_g, ln_b, w_branch_c, w_out, g_post_mix, g_pre_ffn, w_ff1, w_ff2, g_post_ffn):
    B, L, _ = x.shape
    hin = rmsnorm(x, g_pre_mix)
    z = hin @ w_in.astype(hin.dtype)
    f32 = jnp.float32

    def heads(t):
        return t.reshape(B, L, M_HEADS, M_HEAD_DIM).transpose(0, 2, 1, 3).astype(f32)

    q = heads(z[..., OFF_Q:OFF_K])
    k = heads(z[..., OFF_K:OFF_V]) * (M_HEAD_DIM ** -0.5)
    v = heads(z[..., OFF_V:OFF_O])
    o_gate = jax.nn.sigmoid(z[..., OFF_O:OFF_I].astype(f32))
    bif = b_if.astype(f32)
    it = (z[..., OFF_I:OFF_F].astype(f32) + bif[:M_HEADS]).transpose(0, 2, 1)
    lf = jax.nn.log_sigmoid(z[..., OFF_F:OFF_CA].astype(f32) + bif[M_HEADS:]).transpose(0, 2, 1)
    carry = (C0.astype(f32), n0.astype(f32), m0.astype(f32))
    (C1, n1, m1), hm = mlstm_sequence(carry, (q, k, v, it, lf), n_lead)
    mu = jnp.mean(hm, axis=-1, keepdims=True)
    hc = hm - mu
    hm = hc * lax.rsqrt(jnp.mean(hc * hc, axis=-1, keepdims=True) + EPS)
    hm = hm.transpose(0, 2, 1, 3).reshape(B, L, D_M) * g_headnorm.astype(f32) * o_gate
    branch_m = hm.astype(x.dtype) @ w_branch_m.astype(x.dtype)

    u = z[..., OFF_CA:OFF_CB] * jax.nn.sigmoid(z[..., OFF_CB:OFF_GM])
    padded = jnp.concatenate([conv_buf.astype(u.dtype), u], axis=1)
    new_buf = padded[:, -(CONV_W - 1):]
    cv = depthwise_conv_valid(padded, conv_w, conv_b)
    cv = jax.nn.silu(layernorm(cv, ln_g, ln_b))
    branch_c = cv @ w_branch_c.astype(cv.dtype)

    gm = jax.nn.sigmoid(z[..., OFF_GM:OFF_GC])
    gc = jax.nn.sigmoid(z[..., OFF_GC:N_IN])
    merged = gm * branch_m + gc * branch_c
    x = x + rmsnorm(merged @ w_out.astype(merged.dtype), g_post_mix)

    hf = rmsnorm(x, g_pre_ffn)
    hf = jnp.square(jax.nn.relu(hf @ w_ff1.astype(hf.dtype))) @ w_ff2.astype(hf.dtype)
    x = x + rmsnorm(hf, g_post_ffn)
    return x, C1, n1, m1, new_buf


def setup_inputs(seed: int = 0) -> dict:
    key = jax.random.key(seed)
    ks = jax.random.split(key, 24)
    f32 = jnp.float32
    nrm = lambda k, s, sc: jax.random.normal(k, s, f32) * sc
    gain = lambda k, s: 1.0 + 0.05 * jax.random.normal(k, s, f32)
    b_i = 0.1 * jax.random.normal(ks[20], (DEPTH, M_HEADS), f32)
    b_f = jnp.linspace(3.0, 6.0, M_HEADS, dtype=f32)[None, :] + 0.01 * jax.random.normal(ks[21], (DEPTH, M_HEADS), f32)
    return {
        "x_prompt": nrm(ks[0], (BATCH, SEQ, D_MODEL), 1.0),
        "x_sample": nrm(ks[1], (DEC_BATCH, DEC_SEQ, D_MODEL), 1.0),
        "state_mlstm_C": nrm(ks[2], (DEPTH, DEC_BATCH, M_HEADS, M_HEAD_DIM, M_HEAD_DIM), 0.05),
        "state_mlstm_n": nrm(ks[3], (DEPTH, DEC_BATCH, M_HEADS, M_HEAD_DIM), 0.1),
        "state_mlstm_m": nrm(ks[4], (DEPTH, DEC_BATCH, M_HEADS), 0.5),
        "cache_conv": nrm(ks[5], (DEPTH, DEC_BATCH, CONV_W - 1, D_C), 0.5),
        "meta_tokens": nrm(ks[6], (N_META, D_MODEL), 1.0),
        "g_pre_mix": gain(ks[7], (DEPTH, D_MODEL)),
        "w_in": nrm(ks[8], (DEPTH, D_MODEL, N_IN), D_MODEL ** -0.5),
        "b_if": jnp.concatenate([b_i, b_f], axis=-1),
        "g_headnorm": gain(ks[9], (DEPTH, D_M)),
        "w_branch_m": nrm(ks[10], (DEPTH, D_M, D_MODEL), D_M ** -0.5),
        "conv_w": nrm(ks[11], (DEPTH, CONV_W, D_C), CONV_W ** -0.5),
        "conv_b": nrm(ks[12], (DEPTH, D_C), 0.02),
        "ln_g": gain(ks[13], (DEPTH, D_C)),
        "ln_b": nrm(ks[14], (DEPTH, D_C), 0.02),
        "w_branch_c": nrm(ks[15], (DEPTH, D_C, D_MODEL), D_C ** -0.5),
        "w_out": nrm(ks[16], (DEPTH, D_MODEL, D_MODEL), D_MODEL ** -0.5),
        "g_post_mix": gain(ks[17], (DEPTH, D_MODEL)),
        "g_pre_ffn": gain(ks[18], (DEPTH, D_MODEL)),
        "w_ff1": nrm(ks[19], (DEPTH, D_MODEL, D_FF), D_MODEL ** -0.5),
        "w_ff2": nrm(ks[22], (DEPTH, D_FF, D_MODEL), D_FF ** -0.5),
        "g_post_ffn": gain(ks[23], (DEPTH, D_MODEL)),
    }


def reference(x_prompt, x_sample, state_mlstm_C, state_mlstm_n, state_mlstm_m, cache_conv,
              meta_tokens, g_pre_mix, w_in, b_if, g_headnorm, w_branch_m, conv_w, conv_b,
              ln_g, ln_b, w_branch_c, w_out, g_post_mix, g_pre_ffn, w_ff1, w_ff2, g_post_ffn):
    f32 = jnp.float32
    B = x_prompt.shape[0]
    xp = jnp.concatenate([jnp.broadcast_to(meta_tokens.astype(x_prompt.dtype)[None], (B, N_META, D_MODEL)),
                          x_prompt], axis=1)
    xs = x_sample
    Cp, np_, mp, bp = [], [], [], []
    Cs, ns, ms, bs = [], [], [], []
    for l in range(DEPTH):
        w = (g_pre_mix[l], w_in[l], b_if[l], g_headnorm[l], w_branch_m[l], conv_w[l], conv_b[l],
             ln_g[l], ln_b[l], w_branch_c[l], w_out[l], g_post_mix[l], g_pre_ffn[l], w_ff1[l], w_ff2[l],
             g_post_ffn[l])
        zC = jnp.zeros((B, M_HEADS, M_HEAD_DIM, M_HEAD_DIM), f32)
        zn = jnp.zeros((B, M_HEADS, M_HEAD_DIM), f32)
        zm = jnp.zeros((B, M_HEADS), f32)
        zb = jnp.zeros((B, CONV_W - 1, D_C), xp.dtype)
        xp, c1, n1, m1, b1 = layer(xp, zC, zn, zm, zb, N_META, *w)
        Cp.append(c1); np_.append(n1); mp.append(m1); bp.append(b1)
        xs, c2, n2, m2, b2 = layer(xs, state_mlstm_C[l], state_mlstm_n[l], state_mlstm_m[l], cache_conv[l], 0, *w)
        Cs.append(c2); ns.append(n2); ms.append(m2); bs.append(b2)
    y_prompt = xp[:, N_META:]
    y_sample = xs
    C_prompt = jnp.stack(Cp); n_prompt = jnp.stack(np_); m_prompt = jnp.stack(mp); conv_prompt = jnp.stack(bp)
    C_sample = jnp.stack(Cs); n_sample = jnp.stack(ns); m_sample = jnp.stack(ms); conv_sample = jnp.stack(bs)
    return (y_prompt, y_sample, C_prompt, n_prompt, m_prompt, conv_prompt, C_sample, n_sample, m_sample, conv_sample)
```

```python
import functools

import jax
import jax.numpy as jnp
from jax import lax
from jax.experimental import pallas as pl
from jax.experimental.pallas import tpu as pltpu

D_MODEL = 1024
M_HEADS = 4
DH = D_MODEL // M_HEADS
CONV_W = 31
D_FF = 4 * D_MODEL
EPS = 1e-6
CHUNK = 64
N_META = 16

LANES = 128
SUBLANES = 8
HIST = 32
HIST_PAD = HIST - (CONV_W - 1)
CONV_ROWS = 32
VMEM_LIMIT_BYTES = 56 * 1024 * 1024

F32 = jnp.float32
BF16 = jnp.bfloat16


def _sigmoid(x):
    return 1.0 / (1.0 + jnp.exp(-x))


def _log_sigmoid(x):
    return jnp.minimum(x, 0.0) - jnp.log(1.0 + jnp.exp(-jnp.abs(x)))


def _rms(x, g):
    return x * lax.rsqrt(jnp.mean(x * x, axis=-1, keepdims=True) + EPS) * g


def _chunk_cumsum(a, ch):
    pos = lax.broadcasted_iota(jnp.int32, a.shape, 0) & (ch - 1)
    s = 1
    while s < ch:
        a = a + jnp.where(pos >= s, pltpu.roll(a, s, axis=0), 0.0)
        s *= 2
    return a


def _dot(a, b):
    return jnp.dot(a, b, preferred_element_type=F32)


def _dot_nt(a, b):
    return lax.dot_general(a, b, (((1,), (1,)), ((), ())), preferred_element_type=F32)


def _dot_tn(a, b):
    return lax.dot_general(a, b, (((0,), (0,)), ((), ())), preferred_element_type=F32)


def _mixer_kernel(x_ref, c0_ref, n0_ref, m0_ref, u0_ref,
                  gpre_ref, wqkvo_ref, wconv_ref, wmix_ref, wgh_ref, wgl_ref, bif_ref,
                  ghn_ref, wbm_ref, convw_ref, convb_ref, lng_ref, lnb_ref, wbc_ref,
                  wout_ref, gpost_ref,
                  x1_ref, c_out, n_out, m_out, u_out,
                  c_s, n_s, m_s, ubuf, hin_s, q_s, k_s, v_s, og_s, hm_s, cv_s,
                  *, tile, chunk):
    j = pl.program_id(1)
    nj = pl.num_programs(1)

    @pl.when(j == 0)
    def _():
        c_s[...] = c0_ref[0]
        n_s[...] = n0_ref[0]
        m_s[...] = m0_ref[0]
        ubuf[0:HIST, :] = u0_ref[0]

    x = x_ref[0]
    hin = _rms(x, gpre_ref[...])
    hin_hi = hin.astype(BF16)
    hin_lo = (hin - hin_hi.astype(F32)).astype(BF16)
    hin_s[...] = hin_hi

    q_s[...] = _dot(hin_hi, wqkvo_ref[:, 0:D_MODEL]).astype(BF16)
    k_s[...] = _dot(hin_hi, wqkvo_ref[:, D_MODEL:2 * D_MODEL]).astype(BF16)
    v_s[...] = _dot(hin_hi, wqkvo_ref[:, 2 * D_MODEL:3 * D_MODEL]).astype(BF16)
    og_s[...] = _sigmoid(_dot(hin_hi, wqkvo_ref[:, 3 * D_MODEL:4 * D_MODEL]))
    ca = _dot(hin_hi, wconv_ref[:, 0:D_MODEL])
    cb = _dot(hin_hi, wconv_ref[:, D_MODEL:2 * D_MODEL])
    ubuf[HIST:HIST + tile, :] = ca * _sigmoid(cb)

    g = _dot(hin_hi, wgh_ref[...]) + _dot(hin_hi, wgl_ref[...]) + _dot(hin_lo, wgh_ref[...])
    g = g + bif_ref[...]
    lane = lax.broadcasted_iota(jnp.int32, g.shape, 1)
    bcum = _chunk_cumsum(_log_sigmoid(g), chunk)
    gcol = jnp.where(lane < M_HEADS, g, bcum)
    pad_rows = (-tile) % LANES
    if pad_rows:
        gsq = jnp.concatenate([gcol, jnp.zeros((pad_rows, LANES), F32)], axis=0)
    else:
        gsq = gcol
    grow = gsq.T

    tri = (lax.broadcasted_iota(jnp.int32, (chunk, chunk), 0)
           >= lax.broadcasted_iota(jnp.int32, (chunk, chunk), 1))
    inv_sqrt_dh = DH ** -0.5
    for c in range(tile // chunk):
        t0 = c * chunk
        for h in range(M_HEADS):
            hs = slice(h * DH, (h + 1) * DH)
            it_col = gcol[t0:t0 + chunk, h:h + 1]
            b_col = gcol[t0:t0 + chunk, M_HEADS + h:M_HEADS + h + 1]
            it_row = grow[h:h + 1, t0:t0 + chunk]
            b_row = grow[M_HEADS + h:M_HEADS + h + 1, t0:t0 + chunk]
            m0 = m_s[h:h + 1, 0:1]
            n0 = n_s[h:h + 1, :]
            c0 = c_s[h]

            a_col = b_col + m0
            dmat = jnp.where(tri, b_col - b_row + it_row, -jnp.inf)
            m_col = jnp.maximum(a_col, jnp.max(dmat, axis=-1, keepdims=True))
            w_state = jnp.exp(a_col - m_col)
            w_intra = jnp.exp(dmat - m_col)

            q = q_s[t0:t0 + chunk, hs]
            k = k_s[t0:t0 + chunk, hs]
            v = v_s[t0:t0 + chunk, hs]
            p = w_intra * (_dot_nt(q, k) * inv_sqrt_dh)
            num = _dot(p.astype(BF16), v) + w_state * _dot_nt(q, c0.astype(BF16))
            qn = jnp.sum(q.astype(F32) * n0, axis=-1, keepdims=True)
            den = jnp.sum(p, axis=-1, keepdims=True) + w_state * qn
            hh = num / jnp.maximum(jnp.abs(den), jnp.exp(-m_col))

            mu = jnp.mean(hh, axis=-1, keepdims=True)
            hc = hh - mu
            hn = hc * lax.rsqrt(jnp.mean(hc * hc, axis=-1, keepdims=True) + EPS)
            hm_s[t0:t0 + chunk, hs] = (hn * ghn_ref[:, hs] * og_s[t0:t0 + chunk, hs]).astype(BF16)

            m_last = m_col[chunk - 1:chunk, :]
            w_last = w_state[chunk - 1:chunk, :]
            b_last = b_col[chunk - 1:chunk, :]
            ws_col = jnp.exp(b_last - b_col + it_col - m_last)
            wv = (v.astype(F32) * ws_col).astype(BF16)
            c_s[h] = w_last * c0 + _dot_tn(wv, k) * inv_sqrt_dh
            n_s[h:h + 1, :] = w_last * n0 + jnp.sum(ws_col * k.astype(F32), axis=0,
                                                     keepdims=True) * inv_sqrt_dh
            m_s[h:h + 1, :] = jnp.broadcast_to(m_last, (1, LANES))

    branch_m = _dot(hm_s[...], wbm_ref[...])

    rows = min(CONV_ROWS, tile)
    for r0 in range(0, tile, rows):
        acc = jnp.zeros((rows, D_MODEL), F32) + convb_ref[...]
        for tap in range(CONV_W):
            start = r0 + HIST_PAD + tap
            acc = acc + ubuf[start:start + rows, :] * convw_ref[tap:tap + 1, :]
        cv_s[r0:r0 + rows, :] = acc
    ubuf[0:HIST, :] = ubuf[tile:tile + HIST, :]
    cv = cv_s[...]
    mu = jnp.mean(cv, axis=-1, keepdims=True)
    cc = cv - mu
    cn = cc * lax.rsqrt(jnp.mean(cc * cc, axis=-1, keepdims=True) + EPS) * lng_ref[...] + lnb_ref[...]
    cact = cn * _sigmoid(cn)
    branch_c = _dot(cact.astype(BF16), wbc_ref[...])

    hin_hi = hin_s[...]
    gm = _sigmoid(_dot(hin_hi, wmix_ref[:, 0:D_MODEL]))
    gc = _sigmoid(_dot(hin_hi, wmix_ref[:, D_MODEL:2 * D_MODEL]))
    merged = gm * branch_m + gc * branch_c
    mixed = _dot(merged.astype(BF16), wout_ref[...])
    x1_ref[0] = x_ref[0] + _rms(mixed, gpost_ref[...])

    @pl.when(j == nj - 1)
    def _():
        c_out[0] = c_s[...]
        n_out[0] = n_s[...]
        m_out[0] = m_s[...]
        u_out[0] = ubuf[0:HIST, :]


def _ffn_kernel(x_ref, gpre_ref, w1_ref, w2_ref, gpost_ref, y_ref, *, ff_block):
    x = x_ref[...]
    hf = _rms(x, gpre_ref[...]).astype(BF16)
    acc = jnp.zeros(x.shape, F32)
    for f0 in range(0, D_FF, ff_block):
        hid = jnp.maximum(_dot(hf, w1_ref[:, f0:f0 + ff_block]), 0.0)
        acc = acc + _dot((hid * hid).astype(BF16), w2_ref[f0:f0 + ff_block, :])
    y_ref[...] = x + _rms(acc, gpost_ref[...])


def _resident(shape):
    nd = len(shape)
    return pl.BlockSpec(shape, lambda *_: (0,) * nd, pipeline_mode=pl.Buffered(1))


def _mixer(x, c0, n0, m0, u0, w, *, tile, chunk):
    bsz, seq, _ = x.shape
    assert seq % tile == 0 and tile % chunk == 0
    shared = c0.shape[0] == 1
    sidx = (lambda b, j: (0, 0, 0, 0)) if shared else (lambda b, j: (b, 0, 0, 0))
    sidx3 = (lambda b, j: (0, 0, 0)) if shared else (lambda b, j: (b, 0, 0))
    in_specs = [
        pl.BlockSpec((1, tile, D_MODEL), lambda b, j: (b, j, 0)),
        pl.BlockSpec((1, M_HEADS, DH, DH), sidx),
        pl.BlockSpec((1, M_HEADS, DH), sidx3),
        pl.BlockSpec((1, M_HEADS, LANES), sidx3),
        pl.BlockSpec((1, HIST, D_MODEL), sidx3),
    ] + [_resident(a.shape) for a in w]
    out_shape = (
        jax.ShapeDtypeStruct((bsz, seq, D_MODEL), F32),
        jax.ShapeDtypeStruct((bsz, M_HEADS, DH, DH), F32),
        jax.ShapeDtypeStruct((bsz, M_HEADS, DH), F32),
        jax.ShapeDtypeStruct((bsz, M_HEADS, LANES), F32),
        jax.ShapeDtypeStruct((bsz, HIST, D_MODEL), F32),
    )
    out_specs = (
        pl.BlockSpec((1, tile, D_MODEL), lambda b, j: (b, j, 0)),
        pl.BlockSpec((1, M_HEADS, DH, DH), lambda b, j: (b, 0, 0, 0)),
        pl.BlockSpec((1, M_HEADS, DH), lambda b, j: (b, 0, 0)),
        pl.BlockSpec((1, M_HEADS, LANES), lambda b, j: (b, 0, 0)),
        pl.BlockSpec((1, HIST, D_MODEL), lambda b, j: (b, 0, 0)),
    )
    scratch = [
        pltpu.VMEM((M_HEADS, DH, DH), F32),
        pltpu.VMEM((M_HEADS, DH), F32),
        pltpu.VMEM((M_HEADS, LANES), F32),
        pltpu.VMEM((HIST + tile, D_MODEL), F32),
        pltpu.VMEM((tile, D_MODEL), BF16),
        pltpu.VMEM((tile, D_MODEL), BF16),
        pltpu.VMEM((tile, D_MODEL), BF16),
        pltpu.VMEM((tile, D_MODEL), BF16),
        pltpu.VMEM((tile, D_MODEL), F32),
        pltpu.VMEM((tile, D_MODEL), BF16),
        pltpu.VMEM((tile, D_MODEL), F32),
    ]
    return pl.pallas_call(
        functools.partial(_mixer_kernel, tile=tile, chunk=chunk),
        grid=(bsz, seq // tile),
        in_specs=in_specs,
        out_specs=out_specs,
        out_shape=out_shape,
        scratch_shapes=scratch,
        compiler_params=pltpu.CompilerParams(
            dimension_semantics=("arbitrary", "arbitrary"),
            vmem_limit_bytes=VMEM_LIMIT_BYTES),
        name=f"mixer_t{tile}",
    )(x, c0, n0, m0, u0, *w)


def _ffn(x, gpre, w1, w2, gpost, *, tile):
    rows = x.shape[0]
    assert rows % tile == 0
    return pl.pallas_call(
        functools.partial(_ffn_kernel, ff_block=1024),
        grid=(rows // tile,),
        in_specs=[pl.BlockSpec((tile, D_MODEL), lambda i: (i, 0)),
                  _resident(gpre.shape), _resident(w1.shape), _resident(w2.shape),
                  _resident(gpost.shape)],
        out_specs=pl.BlockSpec((tile, D_MODEL), lambda i: (i, 0)),
        out_shape=jax.ShapeDtypeStruct(x.shape, F32),
        compiler_params=pltpu.CompilerParams(
            dimension_semantics=("arbitrary",),
            vmem_limit_bytes=VMEM_LIMIT_BYTES),
        name=f"ffn_t{tile}",
    )(x, gpre, w1, w2, gpost)


def _pick_tile(n, candidates):
    for t in candidates:
        if n % t == 0:
            return t
    return n


def kernel(x_prompt, x_sample, state_mlstm_C, state_mlstm_n, state_mlstm_m, cache_conv, meta_tokens, g_pre_mix, w_in, b_if, g_headnorm, w_branch_m, conv_w, conv_b, ln_g, ln_b, w_branch_c, w_out, g_post_mix, g_pre_ffn, w_ff1, w_ff2, g_post_ffn):
    depth = w_in.shape[0]
    assert depth == 1, "single-layer stack"
    l = 0
    off_i = 4 * D_MODEL
    off_ca = off_i + 2 * M_HEADS
    row = lambda a: a[l].reshape(1, -1).astype(F32)

    wi = w_in[l]
    wqkvo = wi[:, :off_i].astype(BF16)
    wconv = wi[:, off_ca:off_ca + 2 * D_MODEL].astype(BF16)
    wmix = wi[:, off_ca + 2 * D_MODEL:].astype(BF16)
    wg = jnp.pad(wi[:, off_i:off_ca], ((0, 0), (0, LANES - 2 * M_HEADS)))
    wg_hi = wg.astype(BF16)
    wg_lo = (wg - wg_hi.astype(F32)).astype(BF16)
    bif = jnp.pad(b_if[l].astype(F32), (0, LANES - 2 * M_HEADS)).reshape(1, LANES)
    mixer_w = (row(g_pre_mix), wqkvo, wconv, wmix, wg_hi, wg_lo, bif,
               row(g_headnorm), w_branch_m[l].astype(BF16), conv_w[l].astype(F32), row(conv_b),
               row(ln_g), row(ln_b), w_branch_c[l].astype(BF16), w_out[l].astype(BF16),
               row(g_post_mix))
    ffn_w = (row(g_pre_ffn), w_ff1[l].astype(BF16), w_ff2[l].astype(BF16), row(g_post_ffn))

    def pack_state(c, n, m, u):
        m = jnp.broadcast_to(m[..., None], m.shape + (LANES,)).astype(F32)
        u = jnp.pad(u.astype(F32), ((0, 0), (HIST_PAD, 0), (0, 0)))
        return c.astype(F32), n.astype(F32), m, u

    def run(x, state, chunk, tiles):
        bsz, seq, _ = x.shape
        tile = _pick_tile(seq, tiles)
        x1, c1, n1, m1, u1 = _mixer(x, *state, mixer_w, tile=tile, chunk=min(chunk, tile))
        rows = bsz * seq
        y = _ffn(x1.reshape(rows, D_MODEL), *ffn_w, tile=_pick_tile(rows, (512, 256, 128)))
        return y.reshape(bsz, seq, D_MODEL), c1, n1, m1[..., 0], u1[:, HIST_PAD:, :]

    zero_state = pack_state(jnp.zeros((1, M_HEADS, DH, DH), F32), jnp.zeros((1, M_HEADS, DH), F32),
                            jnp.zeros((1, M_HEADS), F32), jnp.zeros((1, CONV_W - 1, D_MODEL), F32))
    _, c_m, n_m, m_m, u_m = _mixer(meta_tokens[None].astype(F32), *zero_state, mixer_w,
                                   tile=N_META, chunk=N_META)
    y_p, c_p, n_p, m_p, u_p = run(x_prompt, (c_m, n_m, m_m, u_m), CHUNK, (256,))

    y_s, c_s, n_s, m_s, u_s = run(
        x_sample, pack_state(state_mlstm_C[l], state_mlstm_n[l], state_mlstm_m[l], cache_conv[l]),
        x_sample.shape[1], (x_sample.shape[1],))

    return (y_p, y_s, c_p[None], n_p[None], m_p[None], u_p[None],
            c_s[None], n_s[None], m_s[None], u_s[None])
```

```python
import functools

import jax
import jax.numpy as jnp
from jax import lax
from jax.experimental import pallas as pl
from jax.experimental.pallas import tpu as pltpu

D_MODEL = 1024
M_HEADS = 4
DH = D_MODEL // M_HEADS
CONV_W = 31
D_FF = 4 * D_MODEL
EPS = 1e-6
N_META = 16

LANES = 128
SUBLANES = 8
HIST = 32
HIST_PAD = HIST - (CONV_W - 1)
CONV_ROWS = 128
CONV_COLS = 256
PROMPT_TILE = 256
SAMPLE_STREAMS = 4
FFN_TILE = 512
VMEM_LIMIT_BYTES = 56 * 1024 * 1024

F32 = jnp.float32
BF16 = jnp.bfloat16


def _sigmoid(x):
    return 1.0 / (1.0 + jnp.exp(-x))


def _log_sigmoid(x):
    return jnp.minimum(x, 0.0) - jnp.log(1.0 + jnp.exp(-jnp.abs(x)))


def _rms(x, g):
    return x * lax.rsqrt(jnp.mean(x * x, axis=-1, keepdims=True) + EPS) * g


def _chunk_cumsum(a, ch):
    pos = lax.broadcasted_iota(jnp.int32, a.shape, 0) & (ch - 1)
    s = 1
    while s < ch:
        a = a + jnp.where(pos >= s, pltpu.roll(a, s, axis=0), 0.0)
        s *= 2
    return a


def _dot(a, b):
    return jnp.dot(a, b, preferred_element_type=F32)


def _dot_nt(a, b):
    return lax.dot_general(a, b, (((1,), (1,)), ((), ())), preferred_element_type=F32)


def _dot_tn(a, b):
    return lax.dot_general(a, b, (((0,), (0,)), ((), ())), preferred_element_type=F32)


def _wt(ref, rows=None, cols=slice(None)):
    rows = slice(None) if rows is None else slice(rows.start // 2, rows.stop // 2)
    return pltpu.bitcast(ref[rows, cols], BF16)


def _conv_block(ubuf, convw_ref, g, r0, rows, l0):
    ls = slice(l0, l0 + LANES)
    acc = None
    for phase in range(SUBLANES):
        extra = SUBLANES if phase else 0
        part = None
        for blk in range(HIST // SUBLANES + 1):
            tap = blk * SUBLANES + phase - HIST_PAD
            if 0 <= tap < CONV_W:
                lo = r0 + blk * SUBLANES
                term = ubuf[g, lo:lo + rows + extra, ls] * convw_ref[tap:tap + 1, ls]
                part = term if part is None else part + term
        part = part[phase:phase + rows]
        acc = part if acc is None else acc + part
    return acc


def _stage_a(x, w, st, hin_o, hm_o, cact_o, *, streams, tile, chunk, between=()):
    pending = list(between)

    def issue_one():
        if pending:
            pending.pop(0)()

    (gpre_ref, wqkvo_ref, wconv_ref, wg_ref, bif_ref, ghn_ref,
     convw_ref, convb_ref, lng_ref, lnb_ref) = w
    c_s, n_s, m_s, ubuf, q_s, k_s, v_s, og_s, cv_s = st
    rows_all = streams * tile

    hin = _rms(x, gpre_ref[...])
    hin_hi = hin.astype(BF16)
    hin_o[...] = hin_hi

    rows = min(CONV_ROWS, tile)
    for col in range(0, D_MODEL, CONV_COLS):
        cs = slice(col, col + CONV_COLS)
        ca = _dot(hin_hi, _wt(wconv_ref, cols=cs))
        cb = _dot(hin_hi, _wt(wconv_ref, cols=slice(D_MODEL + col, D_MODEL + col + CONV_COLS)))
        u = ca * _sigmoid(cb)
        for g in range(streams):
            ubuf[g, HIST:HIST + tile, cs] = u[g * tile:(g + 1) * tile]
            for r0 in range(0, tile, rows):
                for l0 in range(col, col + CONV_COLS, LANES):
                    cv_s[g * tile + r0:g * tile + r0 + rows, l0:l0 + LANES] = (
                        _conv_block(ubuf, convw_ref, g, r0, rows, l0) + convb_ref[:, l0:l0 + LANES])
            ubuf[g, 0:HIST, cs] = ubuf[g, tile:tile + HIST, cs]

    cv = cv_s[...]
    mu = jnp.mean(cv, axis=-1, keepdims=True)
    cc = cv - mu
    cn = cc * lax.rsqrt(jnp.mean(cc * cc, axis=-1, keepdims=True) + EPS) * lng_ref[...] + lnb_ref[...]
    cact_o[...] = (cn * _sigmoid(cn)).astype(BF16)

    gate = _dot(hin_hi, _wt(wg_ref)) + bif_ref[...]
    lane = lax.broadcasted_iota(jnp.int32, gate.shape, 1)
    bcum = _chunk_cumsum(_log_sigmoid(gate), chunk)
    gcol = jnp.where(lane < M_HEADS, gate, bcum)
    pad_rows = (-rows_all) % LANES
    if pad_rows:
        gsq = jnp.concatenate([gcol, jnp.zeros((pad_rows, LANES), F32)], axis=0)
    else:
        gsq = gcol
    grow = gsq.T

    issue_one()
    q_s[...] = _dot(hin_hi, _wt(wqkvo_ref, cols=slice(0, D_MODEL))).astype(BF16)
    k_s[...] = (_dot(hin_hi, _wt(wqkvo_ref, cols=slice(D_MODEL, 2 * D_MODEL))) * DH ** -0.5).astype(BF16)
    v_s[...] = _dot(hin_hi, _wt(wqkvo_ref, cols=slice(2 * D_MODEL, 3 * D_MODEL))).astype(BF16)
    og_s[...] = _sigmoid(_dot(hin_hi, _wt(wqkvo_ref, cols=slice(3 * D_MODEL, 4 * D_MODEL))))

    tri = (lax.broadcasted_iota(jnp.int32, (chunk, chunk), 0)
           >= lax.broadcasted_iota(jnp.int32, (chunk, chunk), 1))
    for g in range(streams):
        for c in range(tile // chunk):
            t0 = g * tile + c * chunk
            ts = slice(t0, t0 + chunk)
            for h in range(M_HEADS):
                hs = slice(h * DH, (h + 1) * DH)
                it_col = gcol[ts, h:h + 1]
                b_col = gcol[ts, M_HEADS + h:M_HEADS + h + 1]
                it_row = grow[h:h + 1, ts]
                b_row = grow[M_HEADS + h:M_HEADS + h + 1, ts]
                m0 = m_s[g, h:h + 1, 0:1]
                n0 = n_s[g, h:h + 1, :]
                c0 = c_s[g, h]

                a_col = b_col + m0
                dmat = jnp.where(tri, b_col + (it_row - b_row), -jnp.inf)
                m_col = jnp.maximum(a_col, jnp.max(dmat, axis=-1, keepdims=True))
                w_state = jnp.exp(a_col - m_col)
                w_intra = jnp.exp(dmat - m_col)

                q = q_s[ts, hs]
                k = k_s[ts, hs]
                v = v_s[ts, hs]
                p = w_intra * _dot_nt(q, k)
                num = _dot(p.astype(BF16), v) + w_state * _dot_nt(q, c0.astype(BF16))
                qn = jnp.sum(q.astype(F32) * n0, axis=-1, keepdims=True)
                den = jnp.sum(p, axis=-1, keepdims=True) + w_state * qn
                hh = num / jnp.maximum(jnp.abs(den), jnp.exp(-m_col))

                mu = jnp.mean(hh, axis=-1, keepdims=True)
                hc = hh - mu
                hn = hc * lax.rsqrt(jnp.mean(hc * hc, axis=-1, keepdims=True) + EPS)
                hm_o[ts, hs] = (hn * ghn_ref[:, hs] * og_s[ts, hs]).astype(BF16)

                m_last = m_col[chunk - 1:chunk, :]
                w_last = w_state[chunk - 1:chunk, :]
                b_last = b_col[chunk - 1:chunk, :]
                ws_col = jnp.exp(b_last - b_col + it_col - m_last)
                wv = (v.astype(F32) * ws_col).astype(BF16)
                c_s[g, h] = _dot_tn(wv, k) + w_last * c0
                n_s[g, h:h + 1, :] = w_last * n0 + jnp.sum(ws_col * k.astype(F32), axis=0,
                                                            keepdims=True)
                m_s[g, h:h + 1, :] = jnp.broadcast_to(m_last, (1, LANES))
                issue_one()
    while pending:
        issue_one()


def _stage_b_steps(x_ref, x1_ref, w, hin_i, hm_i, cact_i):
    wmix_ref, wbm_ref, wbc_ref, wout_ref, gpost_ref = w
    shape = x_ref.shape
    rows = shape[0] * shape[1]
    v = {}

    def gate_m():
        v["gm"] = _sigmoid(_dot(hin_i[...], _wt(wmix_ref, cols=slice(0, D_MODEL))))

    def gate_c():
        v["gc"] = _sigmoid(_dot(hin_i[...], _wt(wmix_ref, cols=slice(D_MODEL, 2 * D_MODEL))))

    def branch_m():
        v["m"] = v["gm"] * _dot(hm_i[...], _wt(wbm_ref))

    def branch_c():
        v["merged"] = (v["m"] + v["gc"] * _dot(cact_i[...], _wt(wbc_ref))).astype(BF16)

    def out():
        mixed = _dot(v["merged"], _wt(wout_ref))
        x1 = x_ref[...].reshape(rows, D_MODEL) + _rms(mixed, gpost_ref[...])
        x1_ref[...] = x1.reshape(shape)

    return [gate_m, gate_c, branch_m, branch_c, out]


N_STATE = 4
N_WA = 10
N_WB = 5
N_OUT = 1 + N_STATE
N_SCRATCH = 9


def _split_refs(refs, n_x):
    sizes = (n_x, N_STATE, N_WA, N_WB, N_OUT, N_SCRATCH, 3)
    parts, i = [], 0
    for n in sizes:
        parts.append(refs[i:i + n])
        i += n
    assert i == len(refs)
    return parts


def _load_state(st, c0_ref, n0_ref, m0_ref, u0_ref):
    c_s, n_s, m_s, ubuf = st[:4]
    c_s[...] = jnp.broadcast_to(c0_ref[...], c_s.shape)
    n_s[...] = jnp.broadcast_to(n0_ref[...], n_s.shape)
    m_s[...] = jnp.broadcast_to(m0_ref[...], m_s.shape)
    ubuf[:, 0:HIST, :] = jnp.broadcast_to(u0_ref[...], (ubuf.shape[0], HIST, D_MODEL))


def _store_state(st, c_out, n_out, m_out, u_out):
    c_s, n_s, m_s, ubuf = st[:4]
    c_out[...] = c_s[...]
    n_out[...] = n_s[...]
    m_out[...] = m_s[...]
    u_out[...] = ubuf[:, 0:HIST, :]


def _mixer_block_kernel(*refs, streams, tile, chunk):
    (x_ref,), state_in, wa, wb, (x1_ref, c_out, n_out, m_out, u_out), st, (hin_s, hm_s, cact_s) = (
        _split_refs(refs, 1))
    rows_all = streams * tile
    _load_state(st, *state_in)
    x = x_ref[...].reshape(rows_all, D_MODEL)
    _stage_a(x, wa, st, hin_s, hm_s, cact_s, streams=streams, tile=tile, chunk=chunk)
    for step in _stage_b_steps(x_ref, x1_ref, wb, hin_s, hm_s, cact_s):
        step()
    _store_state(st, c_out, n_out, m_out, u_out)


def _mixer_pipe_kernel(*refs, tile, tiles_per_stream, n_tiles):
    (xa_ref, xb_ref), state_in, wa, wb, (x1_ref, c_out, n_out, m_out, u_out), st, (hin2, hm2, cact2) = (
        _split_refs(refs, 2))
    s = pl.program_id(0)
    pos = lax.rem(s, tiles_per_stream)

    @pl.when(s == 0)
    def _():
        hin2[...] = jnp.zeros(hin2.shape, BF16)
        hm2[...] = jnp.zeros(hm2.shape, BF16)
        cact2[...] = jnp.zeros(cact2.shape, BF16)

    @pl.when(pos == 0)
    def _():
        _load_state(st, *state_in)

    cur = lax.rem(s, 2)
    prv = 1 - cur
    _stage_a(xa_ref[0], wa, st, hin2.at[cur], hm2.at[cur], cact2.at[cur],
             streams=1, tile=tile, chunk=tile,
             between=_stage_b_steps(xb_ref, x1_ref, wb, hin2.at[prv], hm2.at[prv], cact2.at[prv]))

    @pl.when(jnp.logical_and(pos == tiles_per_stream - 1, s < n_tiles))
    def _():
        _store_state(st, c_out, n_out, m_out, u_out)


def _resident(shape):
    nd = len(shape)
    return pl.BlockSpec(shape, lambda *_: (0,) * nd, pipeline_mode=pl.Buffered(1))


def _state_scratch(streams, tile):
    rows = streams * tile
    return [
        pltpu.VMEM((streams, M_HEADS, DH, DH), F32),
        pltpu.VMEM((streams, M_HEADS, DH), F32),
        pltpu.VMEM((streams, M_HEADS, LANES), F32),
        pltpu.VMEM((streams, HIST + tile, D_MODEL), F32),
        pltpu.VMEM((rows, D_MODEL), BF16),
        pltpu.VMEM((rows, D_MODEL), BF16),
        pltpu.VMEM((rows, D_MODEL), BF16),
        pltpu.VMEM((rows, D_MODEL), F32),
        pltpu.VMEM((rows, D_MODEL), F32),
    ]


def _state_shapes(bsz):
    return (
        jax.ShapeDtypeStruct((bsz, M_HEADS, DH, DH), F32),
        jax.ShapeDtypeStruct((bsz, M_HEADS, DH), F32),
        jax.ShapeDtypeStruct((bsz, M_HEADS, LANES), F32),
        jax.ShapeDtypeStruct((bsz, HIST, D_MODEL), F32),
    )


def _state_specs(sb, imap):
    return [
        pl.BlockSpec((sb, M_HEADS, DH, DH), lambda *i: imap(*i) + (0, 0, 0)),
        pl.BlockSpec((sb, M_HEADS, DH), lambda *i: imap(*i) + (0, 0)),
        pl.BlockSpec((sb, M_HEADS, LANES), lambda *i: imap(*i) + (0, 0)),
        pl.BlockSpec((sb, HIST, D_MODEL), lambda *i: imap(*i) + (0, 0)),
    ]


def _mixer_block(x, state, wa, wb, *, streams):
    bsz, seq, _ = x.shape
    assert bsz % streams == 0
    rows = streams * seq
    xspec = pl.BlockSpec((streams, seq, D_MODEL), lambda b: (b, 0, 0))
    return pl.pallas_call(
        functools.partial(_mixer_block_kernel, streams=streams, tile=seq, chunk=seq),
        grid=(bsz // streams,),
        in_specs=[xspec] + _state_specs(streams, lambda b: (b,))
        + [_resident(a.shape) for a in wa + wb],
        out_specs=[xspec] + _state_specs(streams, lambda b: (b,)),
        out_shape=(jax.ShapeDtypeStruct(x.shape, F32),) + _state_shapes(bsz),
        scratch_shapes=_state_scratch(streams, seq) + [pltpu.VMEM((rows, D_MODEL), BF16)] * 3,
        compiler_params=pltpu.CompilerParams(
            dimension_semantics=("arbitrary",), vmem_limit_bytes=VMEM_LIMIT_BYTES),
        name=f"mixer_block_s{streams}_t{seq}",
    )(x, *state, *wa, *wb)


def _mixer_pipe(x, state, wa, wb, *, tile):
    bsz, seq, _ = x.shape
    assert seq % tile == 0
    tps = seq // tile
    n_tiles = bsz * tps
    last = n_tiles - 1
    amap = lambda s: (jnp.minimum(s, last) // tps, jnp.minimum(s, last) % tps, 0)
    bmap = lambda s: (jnp.maximum(s - 1, 0) // tps, jnp.maximum(s - 1, 0) % tps, 0)
    return pl.pallas_call(
        functools.partial(_mixer_pipe_kernel, tile=tile, tiles_per_stream=tps, n_tiles=n_tiles),
        grid=(n_tiles + 1,),
        in_specs=[pl.BlockSpec((1, tile, D_MODEL), amap), pl.BlockSpec((1, tile, D_MODEL), bmap)]
        + _state_specs(1, lambda s: (0,)) + [_resident(a.shape) for a in wa + wb],
        out_specs=[pl.BlockSpec((1, tile, D_MODEL), bmap)]
        + _state_specs(1, lambda s: (jnp.minimum(s, last) // tps,)),
        out_shape=(jax.ShapeDtypeStruct(x.shape, F32),) + _state_shapes(bsz),
        scratch_shapes=_state_scratch(1, tile) + [pltpu.VMEM((2, tile, D_MODEL), BF16)] * 3,
        compiler_params=pltpu.CompilerParams(
            dimension_semantics=("arbitrary",), vmem_limit_bytes=VMEM_LIMIT_BYTES),
        name=f"mixer_pipe_t{tile}",
    )(x, x, *state, *wa, *wb)


def _ffn_kernel(x_ref, gpre_ref, w1_ref, w2_ref, gpost_ref, y_ref, *, ff_block):
    x = x_ref[...]
    hf = _rms(x, gpre_ref[...]).astype(BF16)
    acc = jnp.zeros(x.shape, F32)
    for f0 in range(0, D_FF, ff_block):
        fs = slice(f0, f0 + ff_block)
        hid = jnp.maximum(_dot(hf, _wt(w1_ref, cols=fs)), 0.0)
        acc = acc + _dot((hid * hid).astype(BF16), _wt(w2_ref, rows=fs))
    y_ref[...] = x + _rms(acc, gpost_ref[...])


def _ffn(x, gpre, w1, w2, gpost, *, tile):
    rows = x.shape[0]
    assert rows % tile == 0
    return pl.pallas_call(
        functools.partial(_ffn_kernel, ff_block=1024),
        grid=(rows // tile,),
        in_specs=[pl.BlockSpec((tile, D_MODEL), lambda i: (i, 0)),
                  _resident(gpre.shape), _resident(w1.shape), _resident(w2.shape),
                  _resident(gpost.shape)],
        out_specs=pl.BlockSpec((tile, D_MODEL), lambda i: (i, 0)),
        out_shape=jax.ShapeDtypeStruct(x.shape, F32),
        compiler_params=pltpu.CompilerParams(
            dimension_semantics=("arbitrary",), vmem_limit_bytes=VMEM_LIMIT_BYTES),
        name=f"ffn_t{tile}",
    )(x, gpre, w1, w2, gpost)


def _pick_tile(n, candidates):
    for t in candidates:
        if n % t == 0:
            return t
    return n


def _pack_rows(w):
    k, n = w.shape
    pairs = jnp.swapaxes(w.astype(BF16).reshape(k // 2, 2, n), 1, 2)
    return lax.bitcast_convert_type(pairs, jnp.uint32)


def kernel(x_prompt, x_sample, state_mlstm_C, state_mlstm_n, state_mlstm_m, cache_conv, meta_tokens, g_pre_mix, w_in, b_if, g_headnorm, w_branch_m, conv_w, conv_b, ln_g, ln_b, w_branch_c, w_out, g_post_mix, g_pre_ffn, w_ff1, w_ff2, g_post_ffn):
    depth = w_in.shape[0]
    assert depth == 1, "single-layer stack"
    l = 0
    off_i = 4 * D_MODEL
    off_ca = off_i + 2 * M_HEADS
    row = lambda a: a[l].reshape(1, -1).astype(F32)

    wi = w_in[l]
    wg = jnp.pad(wi[:, off_i:off_ca], ((0, 0), (0, LANES - 2 * M_HEADS)))
    bif = jnp.pad(b_if[l].astype(F32), (0, LANES - 2 * M_HEADS)).reshape(1, LANES)
    wa = (row(g_pre_mix), _pack_rows(wi[:, :off_i]), _pack_rows(wi[:, off_ca:off_ca + 2 * D_MODEL]),
          _pack_rows(wg), bif, row(g_headnorm),
          conv_w[l].astype(F32), row(conv_b), row(ln_g), row(ln_b))
    wb = (_pack_rows(wi[:, off_ca + 2 * D_MODEL:]), _pack_rows(w_branch_m[l]),
          _pack_rows(w_branch_c[l]), _pack_rows(w_out[l]), row(g_post_mix))
    ffn_w = (row(g_pre_ffn), _pack_rows(w_ff1[l]), _pack_rows(w_ff2[l]), row(g_post_ffn))
    assert len(wa) == N_WA and len(wb) == N_WB

    def pack_state(c, n, m, u):
        m = jnp.broadcast_to(m[..., None], m.shape + (LANES,)).astype(F32)
        u = jnp.pad(u.astype(F32), ((0, 0), (HIST_PAD, 0), (0, 0)))
        return c.astype(F32), n.astype(F32), m, u

    def ffn(x1):
        bsz, seq, _ = x1.shape
        rows = bsz * seq
        y = _ffn(x1.reshape(rows, D_MODEL), *ffn_w, tile=_pick_tile(rows, (FFN_TILE, 256, 128)))
        return y.reshape(bsz, seq, D_MODEL)

    def unpack(c1, n1, m1, u1):
        return c1[None], n1[None], m1[..., 0][None], u1[:, HIST_PAD:, :][None]

    zero_state = pack_state(jnp.zeros((1, M_HEADS, DH, DH), F32), jnp.zeros((1, M_HEADS, DH), F32),
                            jnp.zeros((1, M_HEADS), F32), jnp.zeros((1, CONV_W - 1, D_MODEL), F32))
    _, *meta_state = _mixer_block(meta_tokens[None].astype(F32), zero_state, wa, wb, streams=1)
    seq_p = x_prompt.shape[1]
    x1_p, *state_p = _mixer_pipe(x_prompt, meta_state, wa, wb,
                                 tile=_pick_tile(seq_p, (PROMPT_TILE, 128, 64)))
    y_p = ffn(x1_p)

    state_s = pack_state(state_mlstm_C[l], state_mlstm_n[l], state_mlstm_m[l], cache_conv[l])
    x1_s, *state_s = _mixer_block(x_sample, state_s, wa, wb,
                                  streams=_pick_tile(x_sample.shape[0], (SAMPLE_STREAMS, 2, 1)))
    y_s = ffn(x1_s)

    return (y_p, y_s, *unpack(*state_p), *unpack(*state_s))
```

```python
import functools

import jax
import jax.numpy as jnp
from jax import lax
from jax.experimental import pallas as pl
from jax.experimental.pallas import tpu as pltpu

D_MODEL = 1024
M_HEADS = 4
DH = D_MODEL // M_HEADS
CONV_W = 31
D_FF = 4 * D_MODEL
EPS = 1e-6
N_META = 16

LANES = 128
SUBLANES = 8
HIST = 32
HIST_PAD = HIST - (CONV_W - 1)
CONV_ROWS = 128
CONV_COLS = 256
PROMPT_TILE = 256
SAMPLE_STREAMS = 4
FFN_TILE = 512
VMEM_LIMIT_BYTES = 56 * 1024 * 1024

F32 = jnp.float32
BF16 = jnp.bfloat16


def _sigmoid(x):
    return 1.0 / (1.0 + jnp.exp(-x))


def _log_sigmoid(x):
    return jnp.minimum(x, 0.0) - jnp.log(1.0 + jnp.exp(-jnp.abs(x)))


def _rms(x, g):
    return x * lax.rsqrt(jnp.mean(x * x, axis=-1, keepdims=True) + EPS) * g


def _chunk_cumsum(a, ch):
    pos = lax.broadcasted_iota(jnp.int32, a.shape, 0) & (ch - 1)
    s = 1
    while s < ch:
        a = a + jnp.where(pos >= s, pltpu.roll(a, s, axis=0), 0.0)
        s *= 2
    return a


def _dot(a, b):
    return jnp.dot(a, b, preferred_element_type=F32)


def _dot_nt(a, b):
    return lax.dot_general(a, b, (((1,), (1,)), ((), ())), preferred_element_type=F32)


def _dot_tn(a, b):
    return lax.dot_general(a, b, (((0,), (0,)), ((), ())), preferred_element_type=F32)


def _wt(ref, rows=None, cols=slice(None)):
    rows = slice(None) if rows is None else slice(rows.start // 2, rows.stop // 2)
    return pltpu.bitcast(ref[rows, cols], BF16)


def _conv_block(ubuf, convw_ref, g, r0, rows, l0):
    ls = slice(l0, l0 + LANES)
    acc = None
    for phase in range(SUBLANES):
        extra = SUBLANES if phase else 0
        part = None
        for blk in range(HIST // SUBLANES + 1):
            tap = blk * SUBLANES + phase - HIST_PAD
            if 0 <= tap < CONV_W:
                lo = r0 + blk * SUBLANES
                term = ubuf[g, lo:lo + rows + extra, ls] * convw_ref[tap:tap + 1, ls]
                part = term if part is None else part + term
        part = part[phase:phase + rows]
        acc = part if acc is None else acc + part
    return acc


def _stage_a(x, w, st, hin_o, hm_o, cact_o, *, streams, tile, chunk, between=()):
    pending = list(between)

    def issue_one():
        if pending:
            pending.pop(0)()

    (gpre_ref, wqkvo_ref, wconv_ref, wg_ref, bif_ref, ghn_ref,
     convw_ref, convb_ref, lng_ref, lnb_ref) = w
    c_s, n_s, m_s, ubuf, q_s, k_s, v_s, og_s, cv_s = st
    rows_all = streams * tile

    hin = _rms(x, gpre_ref[...])
    hin_hi = hin.astype(BF16)
    hin_o[...] = hin_hi

    rows = min(CONV_ROWS, tile)
    for col in range(0, D_MODEL, CONV_COLS):
        cs = slice(col, col + CONV_COLS)
        ca = _dot(hin_hi, _wt(wconv_ref, cols=cs))
        cb = _dot(hin_hi, _wt(wconv_ref, cols=slice(D_MODEL + col, D_MODEL + col + CONV_COLS)))
        u = ca * _sigmoid(cb)
        for g in range(streams):
            ubuf[g, HIST:HIST + tile, cs] = u[g * tile:(g + 1) * tile]
            for r0 in range(0, tile, rows):
                for l0 in range(col, col + CONV_COLS, LANES):
                    cv_s[g * tile + r0:g * tile + r0 + rows, l0:l0 + LANES] = (
                        _conv_block(ubuf, convw_ref, g, r0, rows, l0) + convb_ref[:, l0:l0 + LANES])
            ubuf[g, 0:HIST, cs] = ubuf[g, tile:tile + HIST, cs]

    cv = cv_s[...]
    mu = jnp.mean(cv, axis=-1, keepdims=True)
    cc = cv - mu
    cn = cc * lax.rsqrt(jnp.mean(cc * cc, axis=-1, keepdims=True) + EPS) * lng_ref[...] + lnb_ref[...]
    cact_o[...] = (cn * _sigmoid(cn)).astype(BF16)

    gate = _dot(hin_hi, _wt(wg_ref)) + bif_ref[...]
    lane = lax.broadcasted_iota(jnp.int32, gate.shape, 1)
    bcum = _chunk_cumsum(_log_sigmoid(gate), chunk)
    gcol = jnp.where(lane < M_HEADS, gate, bcum)
    pad_rows = (-rows_all) % LANES
    if pad_rows:
        gsq = jnp.concatenate([gcol, jnp.zeros((pad_rows, LANES), F32)], axis=0)
    else:
        gsq = gcol
    grow = gsq.T

    issue_one()
    q_s[...] = _dot(hin_hi, _wt(wqkvo_ref, cols=slice(0, D_MODEL))).astype(BF16)
    k_s[...] = (_dot(hin_hi, _wt(wqkvo_ref, cols=slice(D_MODEL, 2 * D_MODEL))) * DH ** -0.5).astype(BF16)
    v_s[...] = _dot(hin_hi, _wt(wqkvo_ref, cols=slice(2 * D_MODEL, 3 * D_MODEL))).astype(BF16)
    og_s[...] = _sigmoid(_dot(hin_hi, _wt(wqkvo_ref, cols=slice(3 * D_MODEL, 4 * D_MODEL))))

    tri = (lax.broadcasted_iota(jnp.int32, (chunk, chunk), 0)
           >= lax.broadcasted_iota(jnp.int32, (chunk, chunk), 1))
    for g in range(streams):
        for c in range(tile // chunk):
            t0 = g * tile + c * chunk
            ts = slice(t0, t0 + chunk)
            for h in range(M_HEADS):
                hs = slice(h * DH, (h + 1) * DH)
                it_col = gcol[ts, h:h + 1]
                b_col = gcol[ts, M_HEADS + h:M_HEADS + h + 1]
                it_row = grow[h:h + 1, ts]
                b_row = grow[M_HEADS + h:M_HEADS + h + 1, ts]
                m0 = m_s[g, h:h + 1, 0:1]
                n0 = n_s[g, h:h + 1, :]
                c0 = c_s[g, h]

                a_col = b_col + m0
                dmat = jnp.where(tri, b_col + (it_row - b_row), -jnp.inf)
                m_col = jnp.maximum(a_col, jnp.max(dmat, axis=-1, keepdims=True))
                w_state = jnp.exp(a_col - m_col)
                w_intra = jnp.exp(dmat - m_col)

                q = q_s[ts, hs]
                k = k_s[ts, hs]
                v = v_s[ts, hs]
                p = w_intra * _dot_nt(q, k)
                num = _dot(p.astype(BF16), v) + w_state * _dot_nt(q, c0.astype(BF16))
                qn = jnp.sum(q.astype(F32) * n0, axis=-1, keepdims=True)
                den = jnp.sum(p, axis=-1, keepdims=True) + w_state * qn
                hh = num / jnp.maximum(jnp.abs(den), jnp.exp(-m_col))

                mu = jnp.mean(hh, axis=-1, keepdims=True)
                hc = hh - mu
                hn = hc * lax.rsqrt(jnp.mean(hc * hc, axis=-1, keepdims=True) + EPS)
                hm_o[ts, hs] = (hn * ghn_ref[:, hs] * og_s[ts, hs]).astype(BF16)

                m_last = m_col[chunk - 1:chunk, :]
                w_last = w_state[chunk - 1:chunk, :]
                b_last = b_col[chunk - 1:chunk, :]
                ws_col = jnp.exp(b_last - b_col + it_col - m_last)
                wv = (v.astype(F32) * ws_col).astype(BF16)
                c_s[g, h] = _dot_tn(wv, k) + w_last * c0
                n_s[g, h:h + 1, :] = w_last * n0 + jnp.sum(ws_col * k.astype(F32), axis=0,
                                                            keepdims=True)
                m_s[g, h:h + 1, :] = jnp.broadcast_to(m_last, (1, LANES))
                issue_one()
    while pending:
        issue_one()


def _stage_b_steps(x_ref, x1_ref, w, hin_i, hm_i, cact_i):
    wmix_ref, wbm_ref, wbc_ref, wout_ref, gpost_ref = w
    shape = x_ref.shape
    rows = shape[0] * shape[1]
    v = {}

    def gate_m():
        v["gm"] = _sigmoid(_dot(hin_i[...], _wt(wmix_ref, cols=slice(0, D_MODEL))))

    def gate_c():
        v["gc"] = _sigmoid(_dot(hin_i[...], _wt(wmix_ref, cols=slice(D_MODEL, 2 * D_MODEL))))

    def branch_m():
        v["m"] = v["gm"] * _dot(hm_i[...], _wt(wbm_ref))

    def branch_c():
        v["merged"] = (v["m"] + v["gc"] * _dot(cact_i[...], _wt(wbc_ref))).astype(BF16)

    def out():
        mixed = _dot(v["merged"], _wt(wout_ref))
        x1 = x_ref[...].reshape(rows, D_MODEL) + _rms(mixed, gpost_ref[...])
        x1_ref[...] = x1.reshape(shape)

    return [gate_m, gate_c, branch_m, branch_c, out]


N_STATE = 4
N_WA = 10
N_WB = 5
N_OUT = 1 + N_STATE
N_SCRATCH = 9


def _split_refs(refs, n_x):
    sizes = (n_x, N_STATE, N_WA, N_WB, N_OUT, N_SCRATCH, 3)
    parts, i = [], 0
    for n in sizes:
        parts.append(refs[i:i + n])
        i += n
    assert i == len(refs)
    return parts


def _load_state(st, c0_ref, n0_ref, m0_ref, u0_ref):
    c_s, n_s, m_s, ubuf = st[:4]
    c_s[...] = jnp.broadcast_to(c0_ref[...], c_s.shape)
    n_s[...] = jnp.broadcast_to(n0_ref[...], n_s.shape)
    m_s[...] = jnp.broadcast_to(m0_ref[...], m_s.shape)
    ubuf[:, 0:HIST, :] = jnp.broadcast_to(u0_ref[...], (ubuf.shape[0], HIST, D_MODEL))


def _store_state(st, c_out, n_out, m_out, u_out):
    c_s, n_s, m_s, ubuf = st[:4]
    c_out[...] = c_s[...]
    n_out[...] = n_s[...]
    m_out[...] = m_s[...]
    u_out[...] = ubuf[:, 0:HIST, :]


def _mixer_block_kernel(*refs, streams, tile, chunk):
    (x_ref,), state_in, wa, wb, (x1_ref, c_out, n_out, m_out, u_out), st, (hin_s, hm_s, cact_s) = (
        _split_refs(refs, 1))
    rows_all = streams * tile
    _load_state(st, *state_in)
    x = x_ref[...].reshape(rows_all, D_MODEL)
    _stage_a(x, wa, st, hin_s, hm_s, cact_s, streams=streams, tile=tile, chunk=chunk)
    for step in _stage_b_steps(x_ref, x1_ref, wb, hin_s, hm_s, cact_s):
        step()
    _store_state(st, c_out, n_out, m_out, u_out)


def _mixer_pipe_kernel(*refs, tile, tiles_per_stream, n_tiles):
    (xa_ref, xb_ref), state_in, wa, wb, (x1_ref, c_out, n_out, m_out, u_out), st, (hin2, hm2, cact2) = (
        _split_refs(refs, 2))
    s = pl.program_id(0)
    pos = lax.rem(s, tiles_per_stream)

    @pl.when(s == 0)
    def _():
        hin2[...] = jnp.zeros(hin2.shape, BF16)
        hm2[...] = jnp.zeros(hm2.shape, BF16)
        cact2[...] = jnp.zeros(cact2.shape, BF16)

    @pl.when(pos == 0)
    def _():
        _load_state(st, *state_in)

    cur = lax.rem(s, 2)
    prv = 1 - cur
    _stage_a(xa_ref[0], wa, st, hin2.at[cur], hm2.at[cur], cact2.at[cur],
             streams=1, tile=tile, chunk=tile,
             between=_stage_b_steps(xb_ref, x1_ref, wb, hin2.at[prv], hm2.at[prv], cact2.at[prv]))

    @pl.when(jnp.logical_and(pos == tiles_per_stream - 1, s < n_tiles))
    def _():
        _store_state(st, c_out, n_out, m_out, u_out)


def _resident(shape):
    nd = len(shape)
    return pl.BlockSpec(shape, lambda *_: (0,) * nd, pipeline_mode=pl.Buffered(1))


def _state_scratch(streams, tile):
    rows = streams * tile
    return [
        pltpu.VMEM((streams, M_HEADS, DH, DH), F32),
        pltpu.VMEM((streams, M_HEADS, DH), F32),
        pltpu.VMEM((streams, M_HEADS, LANES), F32),
        pltpu.VMEM((streams, HIST + tile, D_MODEL), F32),
        pltpu.VMEM((rows, D_MODEL), BF16),
        pltpu.VMEM((rows, D_MODEL), BF16),
        pltpu.VMEM((rows, D_MODEL), BF16),
        pltpu.VMEM((rows, D_MODEL), F32),
        pltpu.VMEM((rows, D_MODEL), F32),
    ]


def _state_shapes(bsz):
    return (
        jax.ShapeDtypeStruct((bsz, M_HEADS, DH, DH), F32),
        jax.ShapeDtypeStruct((bsz, M_HEADS, DH), F32),
        jax.ShapeDtypeStruct((bsz, M_HEADS, LANES), F32),
        jax.ShapeDtypeStruct((bsz, HIST, D_MODEL), F32),
    )


def _state_specs(sb, imap):
    return [
        pl.BlockSpec((sb, M_HEADS, DH, DH), lambda *i: imap(*i) + (0, 0, 0)),
        pl.BlockSpec((sb, M_HEADS, DH), lambda *i: imap(*i) + (0, 0)),
        pl.BlockSpec((sb, M_HEADS, LANES), lambda *i: imap(*i) + (0, 0)),
        pl.BlockSpec((sb, HIST, D_MODEL), lambda *i: imap(*i) + (0, 0)),
    ]


def _mixer_block(x, state, wa, wb, *, streams):
    bsz, seq, _ = x.shape
    assert bsz % streams == 0
    rows = streams * seq
    xspec = pl.BlockSpec((streams, seq, D_MODEL), lambda b: (b, 0, 0))
    return pl.pallas_call(
        functools.partial(_mixer_block_kernel, streams=streams, tile=seq, chunk=seq),
        grid=(bsz // streams,),
        in_specs=[xspec] + _state_specs(streams, lambda b: (b,))
        + [_resident(a.shape) for a in wa + wb],
        out_specs=[xspec] + _state_specs(streams, lambda b: (b,)),
        out_shape=(jax.ShapeDtypeStruct(x.shape, F32),) + _state_shapes(bsz),
        scratch_shapes=_state_scratch(streams, seq) + [pltpu.VMEM((rows, D_MODEL), BF16)] * 3,
        compiler_params=pltpu.CompilerParams(
            dimension_semantics=("arbitrary",), vmem_limit_bytes=VMEM_LIMIT_BYTES),
        name=f"mixer_block_s{streams}_t{seq}",
    )(x, *state, *wa, *wb)


def _mixer_pipe(x, state, wa, wb, *, tile):
    bsz, seq, _ = x.shape
    assert seq % tile == 0
    tps = seq // tile
    n_tiles = bsz * tps
    last = n_tiles - 1
    amap = lambda s: (jnp.minimum(s, last) // tps, jnp.minimum(s, last) % tps, 0)
    bmap = lambda s: (jnp.maximum(s - 1, 0) // tps, jnp.maximum(s - 1, 0) % tps, 0)
    return pl.pallas_call(
        functools.partial(_mixer_pipe_kernel, tile=tile, tiles_per_stream=tps, n_tiles=n_tiles),
        grid=(n_tiles + 1,),
        in_specs=[pl.BlockSpec((1, tile, D_MODEL), amap), pl.BlockSpec((1, tile, D_MODEL), bmap)]
        + _state_specs(1, lambda s: (0,)) + [_resident(a.shape) for a in wa + wb],
        out_specs=[pl.BlockSpec((1, tile, D_MODEL), bmap)]
        + _state_specs(1, lambda s: (jnp.minimum(s, last) // tps,)),
        out_shape=(jax.ShapeDtypeStruct(x.shape, F32),) + _state_shapes(bsz),
        scratch_shapes=_state_scratch(1, tile) + [pltpu.VMEM((2, tile, D_MODEL), BF16)] * 3,
        compiler_params=pltpu.CompilerParams(
            dimension_semantics=("arbitrary",), vmem_limit_bytes=VMEM_LIMIT_BYTES),
        name=f"mixer_pipe_t{tile}",
    )(x, x, *state, *wa, *wb)


def _ffn_kernel(x_ref, gpre_ref, w1_ref, w2_ref, gpost_ref, y_ref, *, ff_block):
    x = x_ref[...]
    hf = _rms(x, gpre_ref[...]).astype(BF16)
    acc = jnp.zeros(x.shape, F32)
    for f0 in range(0, D_FF, ff_block):
        fs = slice(f0, f0 + ff_block)
        hid = jnp.maximum(_dot(hf, _wt(w1_ref, cols=fs)), 0.0)
        acc = acc + _dot((hid * hid).astype(BF16), _wt(w2_ref, rows=fs))
    y_ref[...] = x + _rms(acc, gpost_ref[...])


def _ffn(x, gpre, w1, w2, gpost, *, tile):
    rows = x.shape[0]
    assert rows % tile == 0
    return pl.pallas_call(
        functools.partial(_ffn_kernel, ff_block=1024),
        grid=(rows // tile,),
        in_specs=[pl.BlockSpec((tile, D_MODEL), lambda i: (i, 0)),
                  _resident(gpre.shape), _resident(w1.shape), _resident(w2.shape),
                  _resident(gpost.shape)],
        out_specs=pl.BlockSpec((tile, D_MODEL), lambda i: (i, 0)),
        out_shape=jax.ShapeDtypeStruct(x.shape, F32),
        compiler_params=pltpu.CompilerParams(
            dimension_semantics=("arbitrary",), vmem_limit_bytes=VMEM_LIMIT_BYTES),
        name=f"ffn_t{tile}",
    )(x, gpre, w1, w2, gpost)


def _pick_tile(n, candidates):
    for t in candidates:
        if n % t == 0:
            return t
    return n


def _pack_kernel(x_ref, *o_refs, cols):
    x = x_ref[...]
    for o_ref, (start, width) in zip(o_refs, cols):
        out_width = o_ref.shape[1]
        piece = x[:, start:start + out_width]
        if width < out_width:
            lane = lax.broadcasted_iota(jnp.int32, piece.shape, 1)
            piece = jnp.where(lane < width, piece, 0.0)
        o_ref[...] = pltpu.bitcast(piece.astype(BF16), jnp.uint32)


def _pack_rows(w, cols=None, *, rows_per_step=128):
    _, k, n = w.shape
    cols = [(0, n)] if cols is None else cols
    assert k % rows_per_step == 0
    widths = [-(-width // LANES) * LANES for _, width in cols]
    return pl.pallas_call(
        functools.partial(_pack_kernel, cols=cols),
        grid=(k // rows_per_step,),
        in_specs=[pl.BlockSpec((None, rows_per_step, n), lambda i: (0, i, 0))],
        out_specs=[pl.BlockSpec((rows_per_step // 2, wd), lambda i: (i, 0)) for wd in widths],
        out_shape=[jax.ShapeDtypeStruct((k // 2, wd), jnp.uint32) for wd in widths],
        compiler_params=pltpu.CompilerParams(dimension_semantics=("arbitrary",)),
        name=f"pack_rows_{k}x{n}",
    )(w)


def kernel(x_prompt, x_sample, state_mlstm_C, state_mlstm_n, state_mlstm_m, cache_conv, meta_tokens, g_pre_mix, w_in, b_if, g_headnorm, w_branch_m, conv_w, conv_b, ln_g, ln_b, w_branch_c, w_out, g_post_mix, g_pre_ffn, w_ff1, w_ff2, g_post_ffn):
    depth = w_in.shape[0]
    assert depth == 1, "single-layer stack"
    l = 0
    off_i = 4 * D_MODEL
    off_ca = off_i + 2 * M_HEADS
    row = lambda a: a[l].reshape(1, -1).astype(F32)

    wqkvo, wg, wconv, wmix = _pack_rows(
        w_in, [(0, off_i), (off_i, 2 * M_HEADS), (off_ca, 2 * D_MODEL), (off_ca + 2 * D_MODEL, 2 * D_MODEL)])
    bif = jnp.pad(b_if[l].astype(F32), (0, LANES - 2 * M_HEADS)).reshape(1, LANES)
    wa = (row(g_pre_mix), wqkvo, wconv, wg, bif, row(g_headnorm),
          conv_w[l].astype(F32), row(conv_b), row(ln_g), row(ln_b))
    wb = (wmix, *_pack_rows(w_branch_m), *_pack_rows(w_branch_c), *_pack_rows(w_out), row(g_post_mix))
    ffn_w = (row(g_pre_ffn), *_pack_rows(w_ff1), *_pack_rows(w_ff2, rows_per_step=512), row(g_post_ffn))
    assert len(wa) == N_WA and len(wb) == N_WB

    def pack_state(c, n, m, u):
        m = jnp.broadcast_to(m[..., None], m.shape + (LANES,)).astype(F32)
        u = jnp.pad(u.astype(F32), ((0, 0), (HIST_PAD, 0), (0, 0)))
        return c.astype(F32), n.astype(F32), m, u

    def ffn(x1):
        bsz, seq, _ = x1.shape
        rows = bsz * seq
        y = _ffn(x1.reshape(rows, D_MODEL), *ffn_w, tile=_pick_tile(rows, (FFN_TILE, 256, 128)))
        return y.reshape(bsz, seq, D_MODEL)

    def unpack(c1, n1, m1, u1):
        return c1[None], n1[None], m1[..., 0][None], u1[:, HIST_PAD:, :][None]

    zero_state = pack_state(jnp.zeros((1, M_HEADS, DH, DH), F32), jnp.zeros((1, M_HEADS, DH), F32),
                            jnp.zeros((1, M_HEADS), F32), jnp.zeros((1, CONV_W - 1, D_MODEL), F32))
    _, *meta_state = _mixer_block(meta_tokens[None].astype(F32), zero_state, wa, wb, streams=1)
    seq_p = x_prompt.shape[1]
    x1_p, *state_p = _mixer_pipe(x_prompt, meta_state, wa, wb,
                                 tile=_pick_tile(seq_p, (PROMPT_TILE, 128, 64)))
    y_p = ffn(x1_p)

    state_s = pack_state(state_mlstm_C[l], state_mlstm_n[l], state_mlstm_m[l], cache_conv[l])
    x1_s, *state_s = _mixer_block(x_sample, state_s, wa, wb,
                                  streams=_pick_tile(x_sample.shape[0], (SAMPLE_STREAMS, 2, 1)))
    y_s = ffn(x1_s)

    return (y_p, y_s, *unpack(*state_p), *unpack(*state_s))
```

```python
import functools

import jax
import jax.numpy as jnp
from jax import lax
from jax.experimental import pallas as pl
from jax.experimental.pallas import tpu as pltpu

D_MODEL = 1024
M_HEADS = 4
DH = D_MODEL // M_HEADS
CONV_W = 31
D_FF = 4 * D_MODEL
EPS = 1e-6
N_META = 16

LANES = 128
SUBLANES = 8
HIST = 32
HIST_PAD = HIST - (CONV_W - 1)
CONV_ROWS = 128
CONV_COLS = 256
PROMPT_TILE = 256
SAMPLE_STREAMS = 4
FFN_TILE = 512
VMEM_LIMIT_BYTES = 56 * 1024 * 1024

F32 = jnp.float32
BF16 = jnp.bfloat16


def _sigmoid(x):
    return 1.0 / (1.0 + jnp.exp(-x))


def _log_sigmoid(x):
    return jnp.minimum(x, 0.0) - jnp.log(1.0 + jnp.exp(-jnp.abs(x)))


def _rms(x, g):
    return x * lax.rsqrt(jnp.mean(x * x, axis=-1, keepdims=True) + EPS) * g


def _chunk_cumsum(a, ch):
    pos = lax.broadcasted_iota(jnp.int32, a.shape, 0) & (ch - 1)
    s = 1
    while s < ch:
        a = a + jnp.where(pos >= s, pltpu.roll(a, s, axis=0), 0.0)
        s *= 2
    return a


def _dot(a, b):
    return jnp.dot(a, b, preferred_element_type=F32)


def _dot_nt(a, b):
    return lax.dot_general(a, b, (((1,), (1,)), ((), ())), preferred_element_type=F32)


def _dot_tn(a, b):
    return lax.dot_general(a, b, (((0,), (0,)), ((), ())), preferred_element_type=F32)


def _wt(ref, rows=None, cols=slice(None)):
    rows = slice(None) if rows is None else slice(rows.start // 2, rows.stop // 2)
    return pltpu.bitcast(ref[rows, cols], BF16)


def _conv_block(ubuf, convw_ref, g, r0, rows, l0):
    ls = slice(l0, l0 + LANES)
    acc = None
    for phase in range(SUBLANES):
        extra = SUBLANES if phase else 0
        part = None
        for blk in range(HIST // SUBLANES + 1):
            tap = blk * SUBLANES + phase - HIST_PAD
            if 0 <= tap < CONV_W:
                lo = r0 + blk * SUBLANES
                term = ubuf[g, lo:lo + rows + extra, ls] * convw_ref[tap:tap + 1, ls]
                part = term if part is None else part + term
        part = part[phase:phase + rows]
        acc = part if acc is None else acc + part
    return acc


def _stage_a(x, w, st, hin_o, hm_o, cact_o, *, streams, tile, chunk, between=()):
    pending = list(between)

    def issue_one():
        if pending:
            pending.pop(0)()

    (gpre_ref, wqkvo_ref, wconv_ref, wg_ref, bif_ref, ghn_ref,
     convw_ref, convb_ref, lng_ref, lnb_ref) = w
    c_s, n_s, m_s, ubuf, q_s, k_s, v_s, og_s, cv_s = st
    rows_all = streams * tile

    hin = _rms(x, gpre_ref[...])
    hin_hi = hin.astype(BF16)
    hin_o[...] = hin_hi

    rows = min(CONV_ROWS, tile)
    for col in range(0, D_MODEL, CONV_COLS):
        cs = slice(col, col + CONV_COLS)
        ca = _dot(hin_hi, _wt(wconv_ref, cols=cs))
        cb = _dot(hin_hi, _wt(wconv_ref, cols=slice(D_MODEL + col, D_MODEL + col + CONV_COLS)))
        u = ca * _sigmoid(cb)
        for g in range(streams):
            ubuf[g, HIST:HIST + tile, cs] = u[g * tile:(g + 1) * tile]
            for r0 in range(0, tile, rows):
                for l0 in range(col, col + CONV_COLS, LANES):
                    cv_s[g * tile + r0:g * tile + r0 + rows, l0:l0 + LANES] = (
                        _conv_block(ubuf, convw_ref, g, r0, rows, l0) + convb_ref[:, l0:l0 + LANES])
            ubuf[g, 0:HIST, cs] = ubuf[g, tile:tile + HIST, cs]

    cv = cv_s[...]
    mu = jnp.mean(cv, axis=-1, keepdims=True)
    cc = cv - mu
    cn = cc * lax.rsqrt(jnp.mean(cc * cc, axis=-1, keepdims=True) + EPS) * lng_ref[...] + lnb_ref[...]
    cact_o[...] = (cn * _sigmoid(cn)).astype(BF16)

    gate = _dot(hin_hi, _wt(wg_ref)) + bif_ref[...]
    lane = lax.broadcasted_iota(jnp.int32, gate.shape, 1)
    bcum = _chunk_cumsum(_log_sigmoid(gate), chunk)
    gcol = jnp.where(lane < M_HEADS, gate, bcum)
    pad_rows = (-rows_all) % LANES
    if pad_rows:
        gsq = jnp.concatenate([gcol, jnp.zeros((pad_rows, LANES), F32)], axis=0)
    else:
        gsq = gcol
    grow = gsq.T

    issue_one()
    q_s[...] = _dot(hin_hi, _wt(wqkvo_ref, cols=slice(0, D_MODEL))).astype(BF16)
    k_s[...] = (_dot(hin_hi, _wt(wqkvo_ref, cols=slice(D_MODEL, 2 * D_MODEL))) * DH ** -0.5).astype(BF16)
    v_s[...] = _dot(hin_hi, _wt(wqkvo_ref, cols=slice(2 * D_MODEL, 3 * D_MODEL))).astype(BF16)
    og_s[...] = _sigmoid(_dot(hin_hi, _wt(wqkvo_ref, cols=slice(3 * D_MODEL, 4 * D_MODEL))))

    tri = (lax.broadcasted_iota(jnp.int32, (chunk, chunk), 0)
           >= lax.broadcasted_iota(jnp.int32, (chunk, chunk), 1))
    for g in range(streams):
        for c in range(tile // chunk):
            t0 = g * tile + c * chunk
            ts = slice(t0, t0 + chunk)
            for h in range(M_HEADS):
                hs = slice(h * DH, (h + 1) * DH)
                it_col = gcol[ts, h:h + 1]
                b_col = gcol[ts, M_HEADS + h:M_HEADS + h + 1]
                it_row = grow[h:h + 1, ts]
                b_row = grow[M_HEADS + h:M_HEADS + h + 1, ts]
                m0 = m_s[g, h:h + 1, 0:1]
                n0 = n_s[g, h:h + 1, :]
                c0 = c_s[g, h]

                a_col = b_col + m0
                dmat = jnp.where(tri, b_col + (it_row - b_row), -jnp.inf)
                m_col = jnp.maximum(a_col, jnp.max(dmat, axis=-1, keepdims=True))
                w_state = jnp.exp(a_col - m_col)
                w_intra = jnp.exp(dmat - m_col)

                q = q_s[ts, hs]
                k = k_s[ts, hs]
                v = v_s[ts, hs]
                p = w_intra * _dot_nt(q, k)
                num = _dot(p.astype(BF16), v) + w_state * _dot_nt(q, c0.astype(BF16))
                qn = jnp.sum(q.astype(F32) * n0, axis=-1, keepdims=True)
                den = jnp.sum(p, axis=-1, keepdims=True) + w_state * qn
                hh = num / jnp.maximum(jnp.abs(den), jnp.exp(-m_col))

                mu = jnp.mean(hh, axis=-1, keepdims=True)
                hc = hh - mu
                hn = hc * lax.rsqrt(jnp.mean(hc * hc, axis=-1, keepdims=True) + EPS)
                hm_o[ts, hs] = (hn * ghn_ref[:, hs] * og_s[ts, hs]).astype(BF16)

                m_last = m_col[chunk - 1:chunk, :]
                w_last = w_state[chunk - 1:chunk, :]
                b_last = b_col[chunk - 1:chunk, :]
                ws_col = jnp.exp(b_last - b_col + it_col - m_last)
                wv = (v.astype(F32) * ws_col).astype(BF16)
                c_s[g, h] = _dot_tn(wv, k) + w_last * c0
                n_s[g, h:h + 1, :] = w_last * n0 + jnp.sum(ws_col * k.astype(F32), axis=0,
                                                            keepdims=True)
                m_s[g, h:h + 1, :] = jnp.broadcast_to(m_last, (1, LANES))
                issue_one()
    while pending:
        issue_one()


def _stage_b_steps(x_ref, x1_ref, w, hin_i, hm_i, cact_i):
    wmix_ref, wbm_ref, wbc_ref, wout_ref, gpost_ref = w
    shape = x_ref.shape
    rows = shape[0] * shape[1]
    v = {}

    def gate_m():
        v["gm"] = _sigmoid(_dot(hin_i[...], _wt(wmix_ref, cols=slice(0, D_MODEL))))

    def gate_c():
        v["gc"] = _sigmoid(_dot(hin_i[...], _wt(wmix_ref, cols=slice(D_MODEL, 2 * D_MODEL))))

    def branch_m():
        v["m"] = v["gm"] * _dot(hm_i[...], _wt(wbm_ref))

    def branch_c():
        v["merged"] = (v["m"] + v["gc"] * _dot(cact_i[...], _wt(wbc_ref))).astype(BF16)

    def out():
        mixed = _dot(v["merged"], _wt(wout_ref))
        x1 = x_ref[...].reshape(rows, D_MODEL) + _rms(mixed, gpost_ref[...])
        x1_ref[...] = x1.reshape(shape)

    return [gate_m, gate_c, branch_m, branch_c, out]


N_STATE = 4
N_WA = 10
N_WB = 5
N_OUT = 1 + N_STATE
N_SCRATCH = 9


def _split_refs(refs, n_x):
    sizes = (n_x, N_STATE, N_WA, N_WB, N_OUT, N_SCRATCH, 3)
    parts, i = [], 0
    for n in sizes:
        parts.append(refs[i:i + n])
        i += n
    assert i == len(refs)
    return parts


def _load_state(st, c0_ref, n0_ref, m0_ref, u0_ref):
    c_s, n_s, m_s, ubuf = st[:4]
    c_s[...] = jnp.broadcast_to(c0_ref[...], c_s.shape)
    n_s[...] = jnp.broadcast_to(n0_ref[...], n_s.shape)
    m_s[...] = jnp.broadcast_to(m0_ref[...], m_s.shape)
    ubuf[:, 0:HIST, :] = jnp.broadcast_to(u0_ref[...], (ubuf.shape[0], HIST, D_MODEL))


def _store_state(st, c_out, n_out, m_out, u_out):
    c_s, n_s, m_s, ubuf = st[:4]
    c_out[...] = c_s[...]
    n_out[...] = n_s[...]
    m_out[...] = m_s[...]
    u_out[...] = ubuf[:, 0:HIST, :]


def _mixer_block_kernel(*refs, streams, tile, chunk):
    (x_ref,), state_in, wa, wb, (x1_ref, c_out, n_out, m_out, u_out), st, (hin_s, hm_s, cact_s) = (
        _split_refs(refs, 1))
    rows_all = streams * tile
    _load_state(st, *state_in)
    x = x_ref[...].reshape(rows_all, D_MODEL)
    _stage_a(x, wa, st, hin_s, hm_s, cact_s, streams=streams, tile=tile, chunk=chunk)
    for step in _stage_b_steps(x_ref, x1_ref, wb, hin_s, hm_s, cact_s):
        step()
    _store_state(st, c_out, n_out, m_out, u_out)


def _mixer_pipe_kernel(*refs, tile, tiles_per_stream, n_tiles):
    (xa_ref, xb_ref), state_in, wa, wb, (x1_ref, c_out, n_out, m_out, u_out), st, (hin2, hm2, cact2) = (
        _split_refs(refs, 2))
    s = pl.program_id(0)
    pos = lax.rem(s, tiles_per_stream)

    @pl.when(s == 0)
    def _():
        hin2[...] = jnp.zeros(hin2.shape, BF16)
        hm2[...] = jnp.zeros(hm2.shape, BF16)
        cact2[...] = jnp.zeros(cact2.shape, BF16)

    @pl.when(pos == 0)
    def _():
        _load_state(st, *state_in)

    cur = lax.rem(s, 2)
    prv = 1 - cur
    _stage_a(xa_ref[0], wa, st, hin2.at[cur], hm2.at[cur], cact2.at[cur],
             streams=1, tile=tile, chunk=tile,
             between=_stage_b_steps(xb_ref, x1_ref, wb, hin2.at[prv], hm2.at[prv], cact2.at[prv]))

    @pl.when(jnp.logical_and(pos == tiles_per_stream - 1, s < n_tiles))
    def _():
        _store_state(st, c_out, n_out, m_out, u_out)


def _resident(shape):
    nd = len(shape)
    return pl.BlockSpec(shape, lambda *_: (0,) * nd, pipeline_mode=pl.Buffered(1))


def _state_scratch(streams, tile):
    rows = streams * tile
    return [
        pltpu.VMEM((streams, M_HEADS, DH, DH), F32),
        pltpu.VMEM((streams, M_HEADS, DH), F32),
        pltpu.VMEM((streams, M_HEADS, LANES), F32),
        pltpu.VMEM((streams, HIST + tile, D_MODEL), F32),
        pltpu.VMEM((rows, D_MODEL), BF16),
        pltpu.VMEM((rows, D_MODEL), BF16),
        pltpu.VMEM((rows, D_MODEL), BF16),
        pltpu.VMEM((rows, D_MODEL), F32),
        pltpu.VMEM((rows, D_MODEL), F32),
    ]


def _state_shapes(bsz):
    return (
        jax.ShapeDtypeStruct((bsz, M_HEADS, DH, DH), F32),
        jax.ShapeDtypeStruct((bsz, M_HEADS, DH), F32),
        jax.ShapeDtypeStruct((bsz, M_HEADS, LANES), F32),
        jax.ShapeDtypeStruct((bsz, HIST, D_MODEL), F32),
    )


def _state_specs(sb, imap):
    return [
        pl.BlockSpec((sb, M_HEADS, DH, DH), lambda *i: imap(*i) + (0, 0, 0)),
        pl.BlockSpec((sb, M_HEADS, DH), lambda *i: imap(*i) + (0, 0)),
        pl.BlockSpec((sb, M_HEADS, LANES), lambda *i: imap(*i) + (0, 0)),
        pl.BlockSpec((sb, HIST, D_MODEL), lambda *i: imap(*i) + (0, 0)),
    ]


def _mixer_block(x, state, wa, wb, *, streams):
    bsz, seq, _ = x.shape
    assert bsz % streams == 0
    rows = streams * seq
    xspec = pl.BlockSpec((streams, seq, D_MODEL), lambda b: (b, 0, 0))
    return pl.pallas_call(
        functools.partial(_mixer_block_kernel, streams=streams, tile=seq, chunk=seq),
        grid=(bsz // streams,),
        in_specs=[xspec] + _state_specs(streams, lambda b: (b,))
        + [_resident(a.shape) for a in wa + wb],
        out_specs=[xspec] + _state_specs(streams, lambda b: (b,)),
        out_shape=(jax.ShapeDtypeStruct(x.shape, F32),) + _state_shapes(bsz),
        scratch_shapes=_state_scratch(streams, seq) + [pltpu.VMEM((rows, D_MODEL), BF16)] * 3,
        compiler_params=pltpu.CompilerParams(
            dimension_semantics=("arbitrary",), vmem_limit_bytes=VMEM_LIMIT_BYTES),
        name=f"mixer_block_s{streams}_t{seq}",
    )(x, *state, *wa, *wb)


def _mixer_pipe(x, state, wa, wb, *, tile):
    bsz, seq, _ = x.shape
    assert seq % tile == 0
    tps = seq // tile
    n_tiles = bsz * tps
    last = n_tiles - 1
    amap = lambda s: (jnp.minimum(s, last) // tps, jnp.minimum(s, last) % tps, 0)
    bmap = lambda s: (jnp.maximum(s - 1, 0) // tps, jnp.maximum(s - 1, 0) % tps, 0)
    return pl.pallas_call(
        functools.partial(_mixer_pipe_kernel, tile=tile, tiles_per_stream=tps, n_tiles=n_tiles),
        grid=(n_tiles + 1,),
        in_specs=[pl.BlockSpec((1, tile, D_MODEL), amap), pl.BlockSpec((1, tile, D_MODEL), bmap)]
        + _state_specs(1, lambda s: (0,)) + [_resident(a.shape) for a in wa + wb],
        out_specs=[pl.BlockSpec((1, tile, D_MODEL), bmap)]
        + _state_specs(1, lambda s: (jnp.minimum(s, last) // tps,)),
        out_shape=(jax.ShapeDtypeStruct(x.shape, F32),) + _state_shapes(bsz),
        scratch_shapes=_state_scratch(1, tile) + [pltpu.VMEM((2, tile, D_MODEL), BF16)] * 3,
        compiler_params=pltpu.CompilerParams(
            dimension_semantics=("arbitrary",), vmem_limit_bytes=VMEM_LIMIT_BYTES),
        name=f"mixer_pipe_t{tile}",
    )(x, x, *state, *wa, *wb)


def _ffn_kernel(x_ref, gpre_ref, w1_ref, w2_ref, gpost_ref, y_ref, *, ff_block):
    x = x_ref[...]
    hf = _rms(x, gpre_ref[...]).astype(BF16)
    acc = jnp.zeros(x.shape, F32)
    for f0 in range(0, D_FF, ff_block):
        fs = slice(f0, f0 + ff_block)
        hid = jnp.maximum(_dot(hf, _wt(w1_ref, cols=fs)), 0.0)
        acc = acc + _dot((hid * hid).astype(BF16), _wt(w2_ref, rows=fs))
    y_ref[...] = x + _rms(acc, gpost_ref[...])


def _ffn(x, gpre, w1, w2, gpost, *, tile):
    rows = x.shape[0]
    assert rows % tile == 0
    return pl.pallas_call(
        functools.partial(_ffn_kernel, ff_block=1024),
        grid=(rows // tile,),
        in_specs=[pl.BlockSpec((tile, D_MODEL), lambda i: (i, 0)),
                  _resident(gpre.shape), _resident(w1.shape), _resident(w2.shape),
                  _resident(gpost.shape)],
        out_specs=pl.BlockSpec((tile, D_MODEL), lambda i: (i, 0)),
        out_shape=jax.ShapeDtypeStruct(x.shape, F32),
        compiler_params=pltpu.CompilerParams(
            dimension_semantics=("arbitrary",), vmem_limit_bytes=VMEM_LIMIT_BYTES),
        name=f"ffn_t{tile}",
    )(x, gpre, w1, w2, gpost)


def _pick_tile(n, candidates):
    for t in candidates:
        if n % t == 0:
            return t
    return n


def _pack_kernel(x_ref, o_ref):
    o_ref[...] = pltpu.bitcast(x_ref[...].astype(BF16), jnp.uint32)


def _pack_rows(w, *, rows_per_step=128):
    _, k, n = w.shape
    assert k % rows_per_step == 0 and n % LANES == 0
    return pl.pallas_call(
        _pack_kernel,
        grid=(k // rows_per_step,),
        in_specs=[pl.BlockSpec((None, rows_per_step, n), lambda i: (0, i, 0))],
        out_specs=pl.BlockSpec((rows_per_step // 2, n), lambda i: (i, 0)),
        out_shape=jax.ShapeDtypeStruct((k // 2, n), jnp.uint32),
        compiler_params=pltpu.CompilerParams(dimension_semantics=("arbitrary",)),
        name=f"pack_rows_{k}x{n}",
    )(w)


def _pack_t_kernel(x_ref, o_ref, *, keep):
    x = x_ref[0].T
    if keep < x.shape[1]:
        lane = lax.broadcasted_iota(jnp.int32, x.shape, 1)
        x = jnp.where(lane < keep, x, 0.0)
    o_ref[...] = pltpu.bitcast(x.astype(BF16), jnp.uint32)


def _pack_cols_t(w_t, start, width, *, block=256):
    _, n, k = w_t.shape
    assert start % SUBLANES == 0
    out_width = -(-width // LANES) * LANES
    block = min(block, out_width)
    assert out_width % block == 0 and start + out_width <= n
    return pl.pallas_call(
        functools.partial(_pack_t_kernel, keep=min(width, block)),
        grid=(out_width // block,),
        in_specs=[pl.BlockSpec((pl.Element(1), pl.Element(block), pl.Element(k)),
                               lambda i: (0, pl.multiple_of(start + i * block, SUBLANES), 0))],
        out_specs=pl.BlockSpec((k // 2, block), lambda i: (0, i)),
        out_shape=jax.ShapeDtypeStruct((k // 2, out_width), jnp.uint32),
        compiler_params=pltpu.CompilerParams(dimension_semantics=("arbitrary",)),
        name=f"pack_cols_t_{start}_{width}",
    )(w_t)


def kernel(x_prompt, x_sample, state_mlstm_C, state_mlstm_n, state_mlstm_m, cache_conv, meta_tokens, g_pre_mix, w_in, b_if, g_headnorm, w_branch_m, conv_w, conv_b, ln_g, ln_b, w_branch_c, w_out, g_post_mix, g_pre_ffn, w_ff1, w_ff2, g_post_ffn):
    depth = w_in.shape[0]
    assert depth == 1, "single-layer stack"
    l = 0
    off_i = 4 * D_MODEL
    off_ca = off_i + 2 * M_HEADS
    row = lambda a: a[l].reshape(1, -1).astype(F32)

    w_in_t = jnp.swapaxes(w_in, 1, 2)
    wqkvo, wg, wconv, wmix = (
        _pack_cols_t(w_in_t, start, width) for start, width in
        [(0, off_i), (off_i, 2 * M_HEADS), (off_ca, 2 * D_MODEL), (off_ca + 2 * D_MODEL, 2 * D_MODEL)])
    bif = jnp.pad(b_if[l].astype(F32), (0, LANES - 2 * M_HEADS)).reshape(1, LANES)
    wa = (row(g_pre_mix), wqkvo, wconv, wg, bif, row(g_headnorm),
          conv_w[l].astype(F32), row(conv_b), row(ln_g), row(ln_b))
    wb = (wmix, _pack_rows(w_branch_m), _pack_rows(w_branch_c), _pack_rows(w_out), row(g_post_mix))
    ffn_w = (row(g_pre_ffn), _pack_rows(w_ff1), _pack_rows(w_ff2, rows_per_step=512), row(g_post_ffn))
    assert len(wa) == N_WA and len(wb) == N_WB

    def pack_state(c, n, m, u):
        m = jnp.broadcast_to(m[..., None], m.shape + (LANES,)).astype(F32)
        u = jnp.pad(u.astype(F32), ((0, 0), (HIST_PAD, 0), (0, 0)))
        return c.astype(F32), n.astype(F32), m, u

    def ffn(x1):
        bsz, seq, _ = x1.shape
        rows = bsz * seq
        y = _ffn(x1.reshape(rows, D_MODEL), *ffn_w, tile=_pick_tile(rows, (FFN_TILE, 256, 128)))
        return y.reshape(bsz, seq, D_MODEL)

    def unpack(c1, n1, m1, u1):
        return c1[None], n1[None], m1[..., 0][None], u1[:, HIST_PAD:, :][None]

    zero_state = pack_state(jnp.zeros((1, M_HEADS, DH, DH), F32), jnp.zeros((1, M_HEADS, DH), F32),
                            jnp.zeros((1, M_HEADS), F32), jnp.zeros((1, CONV_W - 1, D_MODEL), F32))
    _, *meta_state = _mixer_block(meta_tokens[None].astype(F32), zero_state, wa, wb, streams=1)
    seq_p = x_prompt.shape[1]
    x1_p, *state_p = _mixer_pipe(x_prompt, meta_state, wa, wb,
                                 tile=_pick_tile(seq_p, (PROMPT_TILE, 128, 64)))
    y_p = ffn(x1_p)

    state_s = pack_state(state_mlstm_C[l], state_mlstm_n[l], state_mlstm_m[l], cache_conv[l])
    x1_s, *state_s = _mixer_block(x_sample, state_s, wa, wb,
                                  streams=_pick_tile(x_sample.shape[0], (SAMPLE_STREAMS, 2, 1)))
    y_s = ffn(x1_s)

    return (y_p, y_s, *unpack(*state_p), *unpack(*state_s))
```

```python
import functools

import jax
import jax.numpy as jnp
from jax import lax
from jax.experimental import pallas as pl
from jax.experimental.pallas import tpu as pltpu

D_MODEL = 1024
M_HEADS = 4
DH = D_MODEL // M_HEADS
CONV_W = 31
D_FF = 4 * D_MODEL
EPS = 1e-6
N_META = 16

LANES = 128
SUBLANES = 8
HIST = 32
HIST_PAD = HIST - (CONV_W - 1)
CONV_ROWS = 128
CONV_COLS = 256
PROMPT_TILE = 256
SAMPLE_STREAMS = 4
FFN_TILE = 512
VMEM_LIMIT_BYTES = 56 * 1024 * 1024

F32 = jnp.float32
BF16 = jnp.bfloat16


def _sigmoid(x):
    return 1.0 / (1.0 + jnp.exp(-x))


def _log_sigmoid(x):
    return jnp.minimum(x, 0.0) - jnp.log(1.0 + jnp.exp(-jnp.abs(x)))


def _rms(x, g):
    return x * lax.rsqrt(jnp.mean(x * x, axis=-1, keepdims=True) + EPS) * g


def _chunk_cumsum(a, ch):
    pos = lax.broadcasted_iota(jnp.int32, a.shape, 0) & (ch - 1)
    s = 1
    while s < ch:
        a = a + jnp.where(pos >= s, pltpu.roll(a, s, axis=0), 0.0)
        s *= 2
    return a


def _dot(a, b):
    return jnp.dot(a, b, preferred_element_type=F32)


def _dot_nt(a, b):
    return lax.dot_general(a, b, (((1,), (1,)), ((), ())), preferred_element_type=F32)


def _dot_tn(a, b):
    return lax.dot_general(a, b, (((0,), (0,)), ((), ())), preferred_element_type=F32)


def _wt(ref, rows=None, cols=slice(None)):
    rows = slice(None) if rows is None else slice(rows.start // 2, rows.stop // 2)
    return pltpu.bitcast(ref[rows, cols], BF16)


def _conv_block(ubuf, convw_ref, g, r0, rows, l0):
    ls = slice(l0, l0 + LANES)
    acc = None
    for phase in range(SUBLANES):
        extra = SUBLANES if phase else 0
        part = None
        for blk in range(HIST // SUBLANES + 1):
            tap = blk * SUBLANES + phase - HIST_PAD
            if 0 <= tap < CONV_W:
                lo = r0 + blk * SUBLANES
                term = ubuf[g, lo:lo + rows + extra, ls] * convw_ref[tap:tap + 1, ls]
                part = term if part is None else part + term
        part = part[phase:phase + rows]
        acc = part if acc is None else acc + part
    return acc


def _stage_a(x, w, st, hin_o, hm_o, cact_o, *, streams, tile, chunk, between=()):
    pending = list(between)

    def issue_one():
        if pending:
            pending.pop(0)()

    (gpre_ref, wqkvo_ref, wconv_ref, wg_ref, bif_ref, ghn_ref,
     convw_ref, convb_ref, lng_ref, lnb_ref) = w
    c_s, n_s, m_s, ubuf, q_s, k_s, v_s, og_s, cv_s = st
    rows_all = streams * tile

    hin = _rms(x, gpre_ref[...])
    hin_hi = hin.astype(BF16)
    hin_o[...] = hin_hi

    rows = min(CONV_ROWS, tile)
    for col in range(0, D_MODEL, CONV_COLS):
        cs = slice(col, col + CONV_COLS)
        ca = _dot(hin_hi, _wt(wconv_ref, cols=cs))
        cb = _dot(hin_hi, _wt(wconv_ref, cols=slice(D_MODEL + col, D_MODEL + col + CONV_COLS)))
        u = ca * _sigmoid(cb)
        for g in range(streams):
            ubuf[g, HIST:HIST + tile, cs] = u[g * tile:(g + 1) * tile]
            for r0 in range(0, tile, rows):
                for l0 in range(col, col + CONV_COLS, LANES):
                    cv_s[g * tile + r0:g * tile + r0 + rows, l0:l0 + LANES] = (
                        _conv_block(ubuf, convw_ref, g, r0, rows, l0) + convb_ref[:, l0:l0 + LANES])
            ubuf[g, 0:HIST, cs] = ubuf[g, tile:tile + HIST, cs]

    cv = cv_s[...]
    mu = jnp.mean(cv, axis=-1, keepdims=True)
    cc = cv - mu
    cn = cc * lax.rsqrt(jnp.mean(cc * cc, axis=-1, keepdims=True) + EPS) * lng_ref[...] + lnb_ref[...]
    cact_o[...] = (cn * _sigmoid(cn)).astype(BF16)

    gate = _dot(hin_hi, _wt(wg_ref)) + bif_ref[...]
    lane = lax.broadcasted_iota(jnp.int32, gate.shape, 1)
    bcum = _chunk_cumsum(_log_sigmoid(gate), chunk)
    gcol = jnp.where(lane < M_HEADS, gate, bcum)
    pad_rows = (-rows_all) % LANES
    if pad_rows:
        gsq = jnp.concatenate([gcol, jnp.zeros((pad_rows, LANES), F32)], axis=0)
    else:
        gsq = gcol
    grow = gsq.T

    issue_one()
    q_s[...] = _dot(hin_hi, _wt(wqkvo_ref, cols=slice(0, D_MODEL))).astype(BF16)
    k_s[...] = (_dot(hin_hi, _wt(wqkvo_ref, cols=slice(D_MODEL, 2 * D_MODEL))) * DH ** -0.5).astype(BF16)
    v_s[...] = _dot(hin_hi, _wt(wqkvo_ref, cols=slice(2 * D_MODEL, 3 * D_MODEL))).astype(BF16)
    og_s[...] = _sigmoid(_dot(hin_hi, _wt(wqkvo_ref, cols=slice(3 * D_MODEL, 4 * D_MODEL))))

    tri = (lax.broadcasted_iota(jnp.int32, (chunk, chunk), 0)
           >= lax.broadcasted_iota(jnp.int32, (chunk, chunk), 1))
    for g in range(streams):
        for c in range(tile // chunk):
            t0 = g * tile + c * chunk
            ts = slice(t0, t0 + chunk)
            for h in range(M_HEADS):
                hs = slice(h * DH, (h + 1) * DH)
                it_col = gcol[ts, h:h + 1]
                b_col = gcol[ts, M_HEADS + h:M_HEADS + h + 1]
                it_row = grow[h:h + 1, ts]
                b_row = grow[M_HEADS + h:M_HEADS + h + 1, ts]
                m0 = m_s[g, h:h + 1, 0:1]
                n0 = n_s[g, h:h + 1, :]
                c0 = c_s[g, h]

                a_col = b_col + m0
                dmat = jnp.where(tri, b_col + (it_row - b_row), -jnp.inf)
                m_col = jnp.maximum(a_col, jnp.max(dmat, axis=-1, keepdims=True))
                w_state = jnp.exp(a_col - m_col)
                w_intra = jnp.exp(dmat - m_col)

                q = q_s[ts, hs]
                k = k_s[ts, hs]
                v = v_s[ts, hs]
                p = w_intra * _dot_nt(q, k)
                num = _dot(p.astype(BF16), v) + w_state * _dot_nt(q, c0.astype(BF16))
                qn = jnp.sum(q.astype(F32) * n0, axis=-1, keepdims=True)
                den = jnp.sum(p, axis=-1, keepdims=True) + w_state * qn
                hh = num / jnp.maximum(jnp.abs(den), jnp.exp(-m_col))

                mu = jnp.mean(hh, axis=-1, keepdims=True)
                hc = hh - mu
                hn = hc * lax.rsqrt(jnp.mean(hc * hc, axis=-1, keepdims=True) + EPS)
                hm_o[ts, hs] = (hn * ghn_ref[:, hs] * og_s[ts, hs]).astype(BF16)

                m_last = m_col[chunk - 1:chunk, :]
                w_last = w_state[chunk - 1:chunk, :]
                b_last = b_col[chunk - 1:chunk, :]
                ws_col = jnp.exp(b_last - b_col + it_col - m_last)
                wv = (v.astype(F32) * ws_col).astype(BF16)
                c_s[g, h] = _dot_tn(wv, k) + w_last * c0
                n_s[g, h:h + 1, :] = w_last * n0 + jnp.sum(ws_col * k.astype(F32), axis=0,
                                                            keepdims=True)
                m_s[g, h:h + 1, :] = jnp.broadcast_to(m_last, (1, LANES))
                issue_one()
    while pending:
        issue_one()


def _stage_b_steps(x_ref, x1_ref, w, hin_i, hm_i, cact_i):
    wmix_ref, wbm_ref, wbc_ref, wout_ref, gpost_ref = w
    shape = x_ref.shape
    rows = shape[0] * shape[1]
    v = {}

    def gate_m():
        v["gm"] = _sigmoid(_dot(hin_i[...], _wt(wmix_ref, cols=slice(0, D_MODEL))))

    def gate_c():
        v["gc"] = _sigmoid(_dot(hin_i[...], _wt(wmix_ref, cols=slice(D_MODEL, 2 * D_MODEL))))

    def branch_m():
        v["m"] = v["gm"] * _dot(hm_i[...], _wt(wbm_ref))

    def branch_c():
        v["merged"] = (v["m"] + v["gc"] * _dot(cact_i[...], _wt(wbc_ref))).astype(BF16)

    def out():
        mixed = _dot(v["merged"], _wt(wout_ref))
        x1 = x_ref[...].reshape(rows, D_MODEL) + _rms(mixed, gpost_ref[...])
        x1_ref[...] = x1.reshape(shape)

    return [gate_m, gate_c, branch_m, branch_c, out]


N_STATE = 4
N_WA = 10
N_WB = 5
N_OUT = 1 + N_STATE
N_SCRATCH = 9


def _split_refs(refs, n_x):
    sizes = (n_x, N_STATE, N_WA, N_WB, N_OUT, N_SCRATCH, 3)
    parts, i = [], 0
    for n in sizes:
        parts.append(refs[i:i + n])
        i += n
    assert i == len(refs)
    return parts


def _load_state(st, c0_ref, n0_ref, m0_ref, u0_ref):
    c_s, n_s, m_s, ubuf = st[:4]
    c_s[...] = jnp.broadcast_to(c0_ref[...], c_s.shape)
    n_s[...] = jnp.broadcast_to(n0_ref[...], n_s.shape)
    m_s[...] = jnp.broadcast_to(m0_ref[...], m_s.shape)
    ubuf[:, 0:HIST, :] = jnp.broadcast_to(u0_ref[...], (ubuf.shape[0], HIST, D_MODEL))


def _store_state(st, c_out, n_out, m_out, u_out):
    c_s, n_s, m_s, ubuf = st[:4]
    c_out[...] = c_s[...]
    n_out[...] = n_s[...]
    m_out[...] = m_s[...]
    u_out[...] = ubuf[:, 0:HIST, :]


def _mixer_block_kernel(*refs, streams, tile, chunk):
    (x_ref,), state_in, wa, wb, (x1_ref, c_out, n_out, m_out, u_out), st, (hin_s, hm_s, cact_s) = (
        _split_refs(refs, 1))
    rows_all = streams * tile
    _load_state(st, *state_in)
    x = x_ref[...].reshape(rows_all, D_MODEL)
    _stage_a(x, wa, st, hin_s, hm_s, cact_s, streams=streams, tile=tile, chunk=chunk)
    for step in _stage_b_steps(x_ref, x1_ref, wb, hin_s, hm_s, cact_s):
        step()
    _store_state(st, c_out, n_out, m_out, u_out)


def _mixer_pipe_kernel(*refs, tile, tiles_per_stream, n_tiles):
    (xa_ref, xb_ref), state_in, wa, wb, (x1_ref, c_out, n_out, m_out, u_out), st, (hin2, hm2, cact2) = (
        _split_refs(refs, 2))
    s = pl.program_id(0)
    pos = lax.rem(s, tiles_per_stream)

    @pl.when(s == 0)
    def _():
        hin2[...] = jnp.zeros(hin2.shape, BF16)
        hm2[...] = jnp.zeros(hm2.shape, BF16)
        cact2[...] = jnp.zeros(cact2.shape, BF16)

    @pl.when(pos == 0)
    def _():
        _load_state(st, *state_in)

    _stage_a(xa_ref[0], wa, st, hin2.at[0], hm2.at[0], cact2.at[0],
             streams=1, tile=tile, chunk=tile,
             between=_stage_b_steps(xb_ref, x1_ref, wb, hin2.at[1], hm2.at[1], cact2.at[1]))
    hin2[1] = hin2[0]
    hm2[1] = hm2[0]
    cact2[1] = cact2[0]

    @pl.when(jnp.logical_and(pos == tiles_per_stream - 1, s < n_tiles))
    def _():
        _store_state(st, c_out, n_out, m_out, u_out)


def _resident(shape):
    nd = len(shape)
    return pl.BlockSpec(shape, lambda *_: (0,) * nd, pipeline_mode=pl.Buffered(1))


def _state_scratch(streams, tile):
    rows = streams * tile
    return [
        pltpu.VMEM((streams, M_HEADS, DH, DH), F32),
        pltpu.VMEM((streams, M_HEADS, DH), F32),
        pltpu.VMEM((streams, M_HEADS, LANES), F32),
        pltpu.VMEM((streams, HIST + tile, D_MODEL), F32),
        pltpu.VMEM((rows, D_MODEL), BF16),
        pltpu.VMEM((rows, D_MODEL), BF16),
        pltpu.VMEM((rows, D_MODEL), BF16),
        pltpu.VMEM((rows, D_MODEL), F32),
        pltpu.VMEM((rows, D_MODEL), F32),
    ]


def _state_shapes(bsz):
    return (
        jax.ShapeDtypeStruct((bsz, M_HEADS, DH, DH), F32),
        jax.ShapeDtypeStruct((bsz, M_HEADS, DH), F32),
        jax.ShapeDtypeStruct((bsz, M_HEADS, LANES), F32),
        jax.ShapeDtypeStruct((bsz, HIST, D_MODEL), F32),
    )


def _state_specs(sb, imap):
    return [
        pl.BlockSpec((sb, M_HEADS, DH, DH), lambda *i: imap(*i) + (0, 0, 0)),
        pl.BlockSpec((sb, M_HEADS, DH), lambda *i: imap(*i) + (0, 0)),
        pl.BlockSpec((sb, M_HEADS, LANES), lambda *i: imap(*i) + (0, 0)),
        pl.BlockSpec((sb, HIST, D_MODEL), lambda *i: imap(*i) + (0, 0)),
    ]


def _mixer_block(x, state, wa, wb, *, streams):
    bsz, seq, _ = x.shape
    assert bsz % streams == 0
    rows = streams * seq
    xspec = pl.BlockSpec((streams, seq, D_MODEL), lambda b: (b, 0, 0))
    return pl.pallas_call(
        functools.partial(_mixer_block_kernel, streams=streams, tile=seq, chunk=seq),
        grid=(bsz // streams,),
        in_specs=[xspec] + _state_specs(streams, lambda b: (b,))
        + [_resident(a.shape) for a in wa + wb],
        out_specs=[xspec] + _state_specs(streams, lambda b: (b,)),
        out_shape=(jax.ShapeDtypeStruct(x.shape, F32),) + _state_shapes(bsz),
        scratch_shapes=_state_scratch(streams, seq) + [pltpu.VMEM((rows, D_MODEL), BF16)] * 3,
        compiler_params=pltpu.CompilerParams(
            dimension_semantics=("arbitrary",), vmem_limit_bytes=VMEM_LIMIT_BYTES),
        name=f"mixer_block_s{streams}_t{seq}",
    )(x, *state, *wa, *wb)


def _mixer_pipe(x, state, wa, wb, *, tile):
    bsz, seq, _ = x.shape
    assert seq % tile == 0
    tps = seq // tile
    n_tiles = bsz * tps
    last = n_tiles - 1
    amap = lambda s: (jnp.minimum(s, last) // tps, jnp.minimum(s, last) % tps, 0)
    bmap = lambda s: (jnp.maximum(s - 1, 0) // tps, jnp.maximum(s - 1, 0) % tps, 0)
    return pl.pallas_call(
        functools.partial(_mixer_pipe_kernel, tile=tile, tiles_per_stream=tps, n_tiles=n_tiles),
        grid=(n_tiles + 1,),
        in_specs=[pl.BlockSpec((1, tile, D_MODEL), amap), pl.BlockSpec((1, tile, D_MODEL), bmap)]
        + _state_specs(1, lambda s: (0,)) + [_resident(a.shape) for a in wa + wb],
        out_specs=[pl.BlockSpec((1, tile, D_MODEL), bmap)]
        + _state_specs(1, lambda s: (jnp.minimum(s, last) // tps,)),
        out_shape=(jax.ShapeDtypeStruct(x.shape, F32),) + _state_shapes(bsz),
        scratch_shapes=_state_scratch(1, tile) + [pltpu.VMEM((2, tile, D_MODEL), BF16)] * 3,
        compiler_params=pltpu.CompilerParams(
            dimension_semantics=("arbitrary",), vmem_limit_bytes=VMEM_LIMIT_BYTES),
        name=f"mixer_pipe_t{tile}",
    )(x, x, *state, *wa, *wb)


def _ffn_kernel(x_ref, gpre_ref, w1_ref, w2_ref, gpost_ref, y_ref, *, ff_block):
    x = x_ref[...]
    hf = _rms(x, gpre_ref[...]).astype(BF16)
    acc = jnp.zeros(x.shape, F32)
    for f0 in range(0, D_FF, ff_block):
        fs = slice(f0, f0 + ff_block)
        hid = jnp.maximum(_dot(hf, _wt(w1_ref, cols=fs)), 0.0)
        acc = acc + _dot((hid * hid).astype(BF16), _wt(w2_ref, rows=fs))
    y_ref[...] = x + _rms(acc, gpost_ref[...])


def _ffn(x, gpre, w1, w2, gpost, *, tile):
    rows = x.shape[0]
    assert rows % tile == 0
    return pl.pallas_call(
        functools.partial(_ffn_kernel, ff_block=1024),
        grid=(rows // tile,),
        in_specs=[pl.BlockSpec((tile, D_MODEL), lambda i: (i, 0)),
                  _resident(gpre.shape), _resident(w1.shape), _resident(w2.shape),
                  _resident(gpost.shape)],
        out_specs=pl.BlockSpec((tile, D_MODEL), lambda i: (i, 0)),
        out_shape=jax.ShapeDtypeStruct(x.shape, F32),
        compiler_params=pltpu.CompilerParams(
            dimension_semantics=("arbitrary",), vmem_limit_bytes=VMEM_LIMIT_BYTES),
        name=f"ffn_t{tile}",
    )(x, gpre, w1, w2, gpost)


def _pick_tile(n, candidates):
    for t in candidates:
        if n % t == 0:
            return t
    return n


def _pack_kernel(x_ref, o_ref):
    o_ref[...] = pltpu.bitcast(x_ref[...].astype(BF16), jnp.uint32)


def _pack_rows(w, *, rows_per_step=128):
    _, k, n = w.shape
    assert k % rows_per_step == 0 and n % LANES == 0
    return pl.pallas_call(
        _pack_kernel,
        grid=(k // rows_per_step,),
        in_specs=[pl.BlockSpec((None, rows_per_step, n), lambda i: (0, i, 0))],
        out_specs=pl.BlockSpec((rows_per_step // 2, n), lambda i: (i, 0)),
        out_shape=jax.ShapeDtypeStruct((k // 2, n), jnp.uint32),
        compiler_params=pltpu.CompilerParams(dimension_semantics=("arbitrary",)),
        name=f"pack_rows_{k}x{n}",
    )(w)


def _pack_t_kernel(x_ref, o_ref, *, keep):
    x = x_ref[0].T
    if keep < x.shape[1]:
        lane = lax.broadcasted_iota(jnp.int32, x.shape, 1)
        x = jnp.where(lane < keep, x, 0.0)
    o_ref[...] = pltpu.bitcast(x.astype(BF16), jnp.uint32)


def _pack_cols_t(w_t, start, width, *, block=256):
    _, n, k = w_t.shape
    assert start % SUBLANES == 0
    out_width = -(-width // LANES) * LANES
    block = min(block, out_width)
    assert out_width % block == 0 and start + out_width <= n
    return pl.pallas_call(
        functools.partial(_pack_t_kernel, keep=min(width, block)),
        grid=(out_width // block,),
        in_specs=[pl.BlockSpec((pl.Element(1), pl.Element(block), pl.Element(k)),
                               lambda i: (0, pl.multiple_of(start + i * block, SUBLANES), 0))],
        out_specs=pl.BlockSpec((k // 2, block), lambda i: (0, i)),
        out_shape=jax.ShapeDtypeStruct((k // 2, out_width), jnp.uint32),
        compiler_params=pltpu.CompilerParams(dimension_semantics=("arbitrary",)),
        name=f"pack_cols_t_{start}_{width}",
    )(w_t)


def kernel(x_prompt, x_sample, state_mlstm_C, state_mlstm_n, state_mlstm_m, cache_conv, meta_tokens, g_pre_mix, w_in, b_if, g_headnorm, w_branch_m, conv_w, conv_b, ln_g, ln_b, w_branch_c, w_out, g_post_mix, g_pre_ffn, w_ff1, w_ff2, g_post_ffn):
    depth = w_in.shape[0]
    assert depth == 1, "single-layer stack"
    l = 0
    off_i = 4 * D_MODEL
    off_ca = off_i + 2 * M_HEADS
    row = lambda a: a[l].reshape(1, -1).astype(F32)

    w_in_t = jnp.swapaxes(w_in, 1, 2)
    wqkvo, wg, wconv, wmix = (
        _pack_cols_t(w_in_t, start, width) for start, width in
        [(0, off_i), (off_i, 2 * M_HEADS), (off_ca, 2 * D_MODEL), (off_ca + 2 * D_MODEL, 2 * D_MODEL)])
    bif = jnp.pad(b_if[l].astype(F32), (0, LANES - 2 * M_HEADS)).reshape(1, LANES)
    wa = (row(g_pre_mix), wqkvo, wconv, wg, bif, row(g_headnorm),
          conv_w[l].astype(F32), row(conv_b), row(ln_g), row(ln_b))
    wb = (wmix, _pack_rows(w_branch_m), _pack_rows(w_branch_c), _pack_rows(w_out), row(g_post_mix))
    ffn_w = (row(g_pre_ffn), _pack_rows(w_ff1), _pack_rows(w_ff2, rows_per_step=512), row(g_post_ffn))
    assert len(wa) == N_WA and len(wb) == N_WB

    def pack_state(c, n, m, u):
        m = jnp.broadcast_to(m[..., None], m.shape + (LANES,)).astype(F32)
        u = jnp.pad(u.astype(F32), ((0, 0), (HIST_PAD, 0), (0, 0)))
        return c.astype(F32), n.astype(F32), m, u

    def ffn(x1):
        bsz, seq, _ = x1.shape
        rows = bsz * seq
        y = _ffn(x1.reshape(rows, D_MODEL), *ffn_w, tile=_pick_tile(rows, (FFN_TILE, 256, 128)))
        return y.reshape(bsz, seq, D_MODEL)

    def unpack(c1, n1, m1, u1):
        return c1[None], n1[None], m1[..., 0][None], u1[:, HIST_PAD:, :][None]

    zero_state = pack_state(jnp.zeros((1, M_HEADS, DH, DH), F32), jnp.zeros((1, M_HEADS, DH), F32),
                            jnp.zeros((1, M_HEADS), F32), jnp.zeros((1, CONV_W - 1, D_MODEL), F32))
    _, *meta_state = _mixer_block(meta_tokens[None].astype(F32), zero_state, wa, wb, streams=1)
    seq_p = x_prompt.shape[1]
    x1_p, *state_p = _mixer_pipe(x_prompt, meta_state, wa, wb,
                                 tile=_pick_tile(seq_p, (PROMPT_TILE, 128, 64)))
    y_p = ffn(x1_p)

    state_s = pack_state(state_mlstm_C[l], state_mlstm_n[l], state_mlstm_m[l], cache_conv[l])
    x1_s, *state_s = _mixer_block(x_sample, state_s, wa, wb,
                                  streams=_pick_tile(x_sample.shape[0], (SAMPLE_STREAMS, 2, 1)))
    y_s = ffn(x1_s)

    return (y_p, y_s, *unpack(*state_p), *unpack(*state_s))
```
